```python
import functools
import jax
import jax.numpy as jnp
from jax import lax
import numpy as np

D_MODEL = 2048
BATCH = 4
SEQ = 2048
DEPTH = 2
DEC_BATCH = 128
DEC_SEQ = 8
PAST_LEN = 8192
PAGE_SIZE = 128

PLE_DIM = 256
NORM_EPS = 1e-6
A_HEAD = 64
A_HEADS = 16
D_A = A_HEADS * A_HEAD
W_LORA = 96
A_LORA = 96
G_LORA = 256
LNX_EPS = 64e-5
N_A_IN = 3 * D_A + W_LORA + A_LORA + G_LORA
B_HEADS = 8
NOPE_DIM = 128
ROPE_DIM = 64
V_DIM = 128
Q_LORA = 512
KV_LORA = 512
ROPE_THETA = 10000.0
ATTN_SCALE = (NOPE_DIM + ROPE_DIM) ** -0.5
Q_BLOCK = 128
N_B_IN = Q_LORA + KV_LORA + ROPE_DIM
CHUNK = 128
C_GROUPS = 8
C_GROUP_DIM = 128
D_C = C_GROUPS * C_GROUP_DIM
LN_EPS = 1e-5
N_C_IN = 2 * D_C
N_GATE_IN = 3 * D_MODEL
N_IN = N_A_IN + N_B_IN + N_C_IN + N_GATE_IN
D_FF = 4 * D_MODEL

kernel_name = 'hybrid_rwkv7_mla_gmlp_decode_step'


def rms_norm(x, g):
    xf = x.astype(jnp.float32)
    y = xf * lax.rsqrt(jnp.mean(xf * xf, axis=-1, keepdims=True) + NORM_EPS)
    return (y * g.astype(jnp.float32)).astype(x.dtype)


def layer_norm(x, g, b, eps):
    xf = x.astype(jnp.float32)
    xc = xf - jnp.mean(xf, axis=-1, keepdims=True)
    y = xc * lax.rsqrt(jnp.mean(xc * xc, axis=-1, keepdims=True) + eps)
    return (y * g.astype(jnp.float32) + b.astype(jnp.float32)).astype(x.dtype)


def rope_tables(pos):
    inv_freq = ROPE_THETA ** (-jnp.arange(0, ROPE_DIM, 2, dtype=jnp.float32) / ROPE_DIM)
    ang = pos.astype(jnp.float32)[:, None] * inv_freq[None, :]
    return jnp.cos(ang), jnp.sin(ang)


def apply_rope(x, cos, sin):
    xf = x.astype(jnp.float32)
    x1, x2 = jnp.split(xf, 2, axis=-1)
    return jnp.concatenate([x1 * cos - x2 * sin, x1 * sin + x2 * cos], axis=-1).astype(x.dtype)


def rwkv7_mix(za, prev_row, s0, W):
    bsz, t, _ = za.shape
    prev = jnp.concatenate([prev_row[:, None, :].astype(za.dtype), za[:, :-1]], axis=1)
    zs = za + W['rwkv_mu'] * (prev - za)
    splits = np.cumsum([D_A, D_A, D_A, W_LORA, A_LORA]).tolist()
    r, k, v, wl, al, gl = jnp.split(zs, splits, axis=-1)
    w_log = -jax.nn.softplus(-(W['rwkv_w0'] + jnp.tanh(wl) @ W['rwkv_w2'])) - 0.5
    decay = jnp.exp(-jnp.exp(w_log.astype(jnp.float32)))
    a = jax.nn.sigmoid(W['rwkv_a0'] + al @ W['rwkv_a2'])
    gate = jax.nn.sigmoid(gl) @ W['rwkv_g2']

    def heads(u):
        return u.astype(jnp.float32).reshape(bsz, t, A_HEADS, A_HEAD)

    kk = heads(k * W['rwkv_k_k'])
    kk = kk * lax.rsqrt(jnp.maximum(jnp.sum(kk * kk, axis=-1, keepdims=True), 1e-24))
    k = k * (1.0 + (a - 1.0) * W['rwkv_k_a'])
    rh, kh, vh, ah, wh = heads(r), heads(k), heads(v), heads(a), heads(decay)

    def step(s, inp):
        r_t, w_t, k_t, v_t, kk_t, a_t = inp
        sa = jnp.einsum('bhij,bhj->bhi', s, -kk_t)
        s = (s * w_t[:, :, None, :] + sa[:, :, :, None] * (kk_t * a_t)[:, :, None, :]
             + v_t[:, :, :, None] * k_t[:, :, None, :])
        return s, jnp.einsum('bhij,bhj->bhi', s, r_t)

    def to_t(u):
        return jnp.swapaxes(u, 0, 1)

    s_last, y = lax.scan(step, s0.astype(jnp.float32),
                         (to_t(rh), to_t(wh), to_t(kh), to_t(vh), to_t(kk), to_t(ah)))
    y = to_t(y)
    yc = y - jnp.mean(y, axis=-1, keepdims=True)
    yn = (yc * lax.rsqrt(jnp.mean(yc * yc, axis=-1, keepdims=True) + LNX_EPS)).reshape(bsz, t, D_A)
    yn = yn * W['rwkv_lnx_g'].astype(jnp.float32) + W['rwkv_lnx_b'].astype(jnp.float32)
    bonus = (jnp.sum(rh * kh * W['rwkv_r_k'].astype(jnp.float32), axis=-1, keepdims=True) * vh).reshape(bsz, t, D_A)
    o = (yn + bonus).astype(za.dtype) * gate
    return o @ W['w_out_a'], s_last.astype(s0.dtype), za[:, -1]


def mla_project(zb, pos, W):
    cq, ckv, kr = jnp.split(zb, [Q_LORA, Q_LORA + KV_LORA], axis=-1)
    cq = rms_norm(cq, W['mla_q_norm'])
    q = jnp.einsum('btc,chd->bthd', cq, W['mla_w_uq'])
    q_nope, q_rope = q[..., :NOPE_DIM], q[..., NOPE_DIM:]
    cos, sin = rope_tables(pos)
    q_rope = apply_rope(q_rope, cos[:, None, :], sin[:, None, :])
    kr = apply_rope(kr, cos, sin)
    ckv = rms_norm(ckv, W['mla_kv_norm'])
    q_lat = jnp.einsum('bthn,chn->bthc', q_nope, W['mla_w_uk'])
    return q_lat, q_rope, ckv, kr


def latent_attention(q_lat, q_rope, c_kv, k_rope, q_pos, k_pos):
    s = (jnp.einsum('...qhc,...kc->...hqk', q_lat, c_kv)
         + jnp.einsum('...qhr,...kr->...hqk', q_rope, k_rope)).astype(jnp.float32) * ATTN_SCALE
    s = jnp.where(k_pos[None, :] <= q_pos[:, None], s, jnp.finfo(jnp.float32).min)
    p = jax.nn.softmax(s, axis=-1).astype(c_kv.dtype)
    return jnp.einsum('...hqk,...kc->...qhc', p, c_kv)


def prompt_attend(q_lat, q_rope, c_kv, k_rope):
    bsz, t, h, c = q_lat.shape
    nb = t // Q_BLOCK

    def blocks(u):
        return jnp.swapaxes(u.reshape((bsz, nb, Q_BLOCK) + u.shape[2:]), 0, 1)

    k_pos = jnp.arange(t)

    def one(args):
        ql, qr, qp = args
        return latent_attention(ql, qr, c_kv, k_rope, qp, k_pos)

    o = lax.map(one, (blocks(q_lat), blocks(q_rope), k_pos.reshape(nb, Q_BLOCK)))
    return jnp.swapaxes(o, 0, 1).reshape(bsz, t, h, c)


def paged_attend(q_lat, q_rope, c_new, r_new, cache_c, cache_r, page_table, layer):
    n_q = q_lat.shape[1]
    past = page_table.shape[1] * PAGE_SIZE
    q_pos = past + jnp.arange(n_q)
    k_pos = jnp.arange(past + n_q)

    def one(args):
        ql, qr, cn, rn, pages = args
        pc = cache_c[layer, pages].reshape(past, KV_LORA).astype(cn.dtype)
        pr = cache_r[layer, pages].reshape(past, ROPE_DIM).astype(rn.dtype)
        return latent_attention(ql, qr, jnp.concatenate([pc, cn], axis=0),
                                jnp.concatenate([pr, rn], axis=0), q_pos, k_pos)

    return lax.map(one, (q_lat, q_rope, c_new, r_new, page_table))


def chunk_spatial_gate(v, w_s, b_s):
    bsz, t, _ = v.shape
    tp = -(-t // CHUNK) * CHUNK
    vr = jnp.pad(v, ((0, 0), (0, tp - t), (0, 0))).reshape(bsz, tp // CHUNK, CHUNK, C_GROUPS, C_GROUP_DIM)
    w_m = jnp.where(jnp.tril(jnp.ones((CHUNK, CHUNK), dtype=bool)), w_s, jnp.zeros_like(w_s))
    s = jnp.einsum('gts,bnsgc->bntgc', w_m, vr) + jnp.swapaxes(b_s, 0, 1)[:, :, None]
    return s.reshape(bsz, tp, D_C)[:, :t]


def gmlp_mix(zc, W):
    zc = jax.nn.gelu(zc, approximate=False)
    u, v = jnp.split(zc, 2, axis=-1)
    v = layer_norm(v, W['gmlp_ln_g'], W['gmlp_ln_b'], LN_EPS)
    return (u * chunk_spatial_gate(v, W['gmlp_w_s'], W['gmlp_b_s'])) @ W['w_out_c'], v


def trunk_layer(x, ple, pos, s0, prev_row, attend, W):
    h = rms_norm(x, W['norm_mix_pre'])
    z = h @ W['w_in']
    za, zb, zc, zg = jnp.split(z, [N_A_IN, N_A_IN + N_B_IN, N_A_IN + N_B_IN + N_C_IN], axis=-1)
    o_a, s_last, z_last = rwkv7_mix(za, prev_row, s0, W)
    q_lat, q_rope, ckv, kr = mla_project(zb, pos, W)
    o_lat = attend(q_lat, q_rope, ckv, kr)
    o_b = jnp.einsum('bthc,chv->bthv', o_lat, W['mla_w_uv'])
    o_b = o_b.reshape(o_b.shape[0], o_b.shape[1], B_HEADS * V_DIM) @ W['w_out_b']
    o_c, v_c = gmlp_mix(zc, W)
    g_a, g_b, g_c = jnp.split(jax.nn.sigmoid(zg), 3, axis=-1)
    m = g_a * o_a + g_b * o_b + g_c * o_c
    x = x + rms_norm(m @ W['w_o'], W['norm_mix_post'])
    hf = rms_norm(x, W['norm_ffn_pre'])
    f = jnp.square(jax.nn.relu(hf @ W['ffn_up'])) @ W['ffn_down']
    x = x + rms_norm(f, W['norm_ffn_post'])
    x = x + jax.nn.sigmoid(x @ W['ple_gate']) * (ple @ W['ple_proj'])
    return x, s_last, z_last, ckv, kr, v_c


def setup_inputs(seed: int = 0) -> dict:
    key = jax.random.key(seed)
    keys = iter(jax.random.split(key, 64))
    f32 = jnp.float32
    L = DEPTH

    def normal(shape, scale=1.0):
        return scale * jax.random.normal(next(keys), shape, f32)

    def gain(shape):
        return 1.0 + 0.05 * jax.random.normal(next(keys), shape, f32)

    def unif(shape, lo, hi):
        return jax.random.uniform(next(keys), shape, f32, lo, hi)

    n_pages = PAST_LEN // PAGE_SIZE
    n_pool = (5 * DEC_BATCH * n_pages) // 4
    page_table = jax.random.permutation(next(keys), n_pool)[: DEC_BATCH * n_pages]
    page_table = page_table.reshape(DEC_BATCH, n_pages).astype(jnp.int32)
    return {
        'x_prompt': normal((BATCH, SEQ, D_MODEL)),
        'x_sample': normal((DEC_BATCH, DEC_SEQ, D_MODEL)),
        'state_rwkv': normal((L, DEC_BATCH, A_HEADS, A_HEAD, A_HEAD), 0.3),
        'state_rwkv_shift': normal((L, DEC_BATCH, N_A_IN)),
        'cache_ckv': normal((L, n_pool, PAGE_SIZE, KV_LORA)),
        'cache_kr': normal((L, n_pool, PAGE_SIZE, ROPE_DIM)),
        'page_table': page_table,
        'p_prompt': normal((L, BATCH, SEQ, PLE_DIM)),
        'p_sample': normal((L, DEC_BATCH, DEC_SEQ, PLE_DIM)),
        'norm_mix_pre': gain((L, D_MODEL)),
        'norm_mix_post': gain((L, D_MODEL)),
        'norm_ffn_pre': gain((L, D_MODEL)),
        'norm_ffn_post': gain((L, D_MODEL)),
        'w_in': normal((L, D_MODEL, N_IN), D_MODEL ** -0.5),
        'rwkv_mu': unif((L, N_A_IN), 0.0, 1.0),
        'rwkv_w0': unif((L, D_A), -6.0, -1.0),
        'rwkv_w2': normal((L, W_LORA, D_A), 0.1 * W_LORA ** -0.5),
        'rwkv_a0': normal((L, D_A), 0.1),
        'rwkv_a2': normal((L, A_LORA, D_A), 0.1 * A_LORA ** -0.5),
        'rwkv_g2': normal((L, G_LORA, D_A), G_LORA ** -0.5),
        'rwkv_k_k': 0.85 + normal((L, D_A), 0.05),
        'rwkv_k_a': gain((L, D_A)),
        'rwkv_r_k': normal((L, A_HEADS, A_HEAD), 0.1),
        'rwkv_lnx_g': gain((L, D_A)),
        'rwkv_lnx_b': normal((L, D_A), 0.02),
        'w_out_a': normal((L, D_A, D_MODEL), D_A ** -0.5),
        'mla_q_norm': gain((L, Q_LORA)),
        'mla_w_uq': normal((L, Q_LORA, B_HEADS, NOPE_DIM + ROPE_DIM), Q_LORA ** -0.5),
        'mla_kv_norm': gain((L, KV_LORA)),
        'mla_w_uk': normal((L, KV_LORA, B_HEADS, NOPE_DIM), KV_LORA ** -0.5),
        'mla_w_uv': normal((L, KV_LORA, B_HEADS, V_DIM), KV_LORA ** -0.5),
        'w_out_b': normal((L, B_HEADS * V_DIM, D_MODEL), (B_HEADS * V_DIM) ** -0.5),
        'gmlp_ln_g': gain((L, D_C)),
        'gmlp_ln_b': normal((L, D_C), 0.02),
        'gmlp_w_s': normal((L, C_GROUPS, CHUNK, CHUNK), CHUNK ** -0.5),
        'gmlp_b_s': gain((L, C_GROUPS, CHUNK)),
        'w_out_c': normal((L, D_C, D_MODEL), D_C ** -0.5),
        'w_o': normal((L, D_MODEL, D_MODEL), D_MODEL ** -0.5),
        'ffn_up': normal((L, D_MODEL, D_FF), D_MODEL ** -0.5),
        'ffn_down': normal((L, D_FF, D_MODEL), D_FF ** -0.5),
        'ple_proj': normal((L, PLE_DIM, D_MODEL), PLE_DIM ** -0.5),
        'ple_gate': normal((L, D_MODEL, D_MODEL), D_MODEL ** -0.5),
    }


def reference(x_prompt, x_sample, state_rwkv, state_rwkv_shift, cache_ckv, cache_kr, page_table,
              p_prompt, p_sample, norm_mix_pre, norm_mix_post, norm_ffn_pre, norm_ffn_post, w_in,
              rwkv_mu, rwkv_w0, rwkv_w2, rwkv_a0, rwkv_a2, rwkv_g2, rwkv_k_k, rwkv_k_a, rwkv_r_k,
              rwkv_lnx_g, rwkv_lnx_b, w_out_a, mla_q_norm, mla_w_uq, mla_kv_norm, mla_w_uk, mla_w_uv,
              w_out_b, gmlp_ln_g, gmlp_ln_b, gmlp_w_s, gmlp_b_s, w_out_c, w_o, ffn_up, ffn_down,
              ple_proj, ple_gate):
    bsz = x_prompt.shape[0]
    past_len = page_table.shape[1] * PAGE_SIZE
    pos_p = jnp.arange(x_prompt.shape[1])
    pos_s = past_len + jnp.arange(x_sample.shape[1])
    s0_p = jnp.zeros((bsz, A_HEADS, A_HEAD, A_HEAD), x_prompt.dtype)
    prev_p = jnp.zeros((bsz, N_A_IN), x_prompt.dtype)
    xp, xs = x_prompt, x_sample
    st_p, sh_p, ckv_p, kr_p = [], [], [], []
    st_s, sh_s, ckv_s, kr_s, vc_s = [], [], [], [], []
    for i in range(DEPTH):
        W = {
            'norm_mix_pre': norm_mix_pre[i], 'norm_mix_post': norm_mix_post[i],
            'norm_ffn_pre': norm_ffn_pre[i], 'norm_ffn_post': norm_ffn_post[i], 'w_in': w_in[i],
            'rwkv_mu': rwkv_mu[i], 'rwkv_w0': rwkv_w0[i], 'rwkv_w2': rwkv_w2[i], 'rwkv_a0': rwkv_a0[i],
            'rwkv_a2': rwkv_a2[i], 'rwkv_g2': rwkv_g2[i], 'rwkv_k_k': rwkv_k_k[i], 'rwkv_k_a': rwkv_k_a[i],
            'rwkv_r_k': rwkv_r_k[i], 'rwkv_lnx_g': rwkv_lnx_g[i], 'rwkv_lnx_b': rwkv_lnx_b[i],
            'w_out_a': w_out_a[i], 'mla_q_norm': mla_q_norm[i], 'mla_w_uq': mla_w_uq[i],
            'mla_kv_norm': mla_kv_norm[i], 'mla_w_uk': mla_w_uk[i], 'mla_w_uv': mla_w_uv[i],
            'w_out_b': w_out_b[i], 'gmlp_ln_g': gmlp_ln_g[i], 'gmlp_ln_b': gmlp_ln_b[i],
            'gmlp_w_s': gmlp_w_s[i], 'gmlp_b_s': gmlp_b_s[i], 'w_out_c': w_out_c[i], 'w_o': w_o[i],
            'ffn_up': ffn_up[i], 'ffn_down': ffn_down[i], 'ple_proj': ple_proj[i], 'ple_gate': ple_gate[i],
        }
        xp, s_a, z_a, c_a, r_a, _ = trunk_layer(xp, p_prompt[i], pos_p, s0_p, prev_p, prompt_attend, W)
        st_p.append(s_a)
        sh_p.append(z_a)
        ckv_p.append(c_a)
        kr_p.append(r_a)
        attend_s = functools.partial(paged_attend, cache_c=cache_ckv, cache_r=cache_kr,
                                     page_table=page_table, layer=i)
        xs, s_b, z_b, c_b, r_b, v_b = trunk_layer(xs, p_sample[i], pos_s, state_rwkv[i],
                                                   state_rwkv_shift[i], attend_s, W)
        st_s.append(s_b)
        sh_s.append(z_b)
        ckv_s.append(c_b)
        kr_s.append(r_b)
        vc_s.append(v_b)
    return (xp, xs, jnp.stack(st_p), jnp.stack(sh_p), jnp.stack(ckv_p), jnp.stack(kr_p),
            jnp.stack(st_s), jnp.stack(sh_s), jnp.stack(ckv_s), jnp.stack(kr_s), jnp.stack(vc_s))
```

```python
import functools

import jax
import jax.numpy as jnp
import numpy as np
from jax import lax
from jax.experimental import pallas as pl
from jax.experimental.pallas import tpu as pltpu

F32 = jnp.float32
BF16 = jnp.bfloat16

NORM_EPS = 1e-6
LNX_EPS = 64e-5
LN_EPS = 1e-5
ROPE_THETA = 10000.0
PAGE_SIZE = 128
CHUNK = 128
A_HEAD = 64
W_LORA = 96
A_LORA = 96
G_LORA = 256
LANE = 128
VMEM_LIMIT = 56 * 1024 * 1024


def _cparams(*sem):
    return pltpu.CompilerParams(dimension_semantics=sem, vmem_limit_bytes=VMEM_LIMIT)


def _pick_tile(m, target, quantum=LANE):
    t = min(target, m)
    t -= t % quantum
    while m % t:
        t -= quantum
    return t


def _rms(x, g):
    return x * lax.rsqrt(jnp.mean(x * x, axis=-1, keepdims=True) + NORM_EPS) * g


def _bdot(a, b):
    return jnp.dot(a.astype(BF16), b.astype(BF16), preferred_element_type=F32)


def _seg_sum(x, ones_bd):
    outs = []
    for c in range(x.shape[1] // LANE):
        xb = x[:, c * LANE:(c + 1) * LANE]
        hi = xb.astype(BF16)
        r1 = xb - hi.astype(F32)
        mid = r1.astype(BF16)
        lo = (r1 - mid.astype(F32)).astype(BF16)
        s = (jnp.dot(hi, ones_bd, preferred_element_type=F32)
             + jnp.dot(mid, ones_bd, preferred_element_type=F32)
             + jnp.dot(lo, ones_bd, preferred_element_type=F32))
        outs.append(s)
    return jnp.concatenate(outs, axis=1)


def _norm_mm_kernel(x_ref, g_ref, w_ref, o_ref, xn_ref, *, act):
    @pl.when(pl.program_id(1) == 0)
    def _():
        xn_ref[...] = _rms(x_ref[...], g_ref[...]).astype(BF16)

    y = jnp.dot(xn_ref[...], w_ref[...].astype(BF16), preferred_element_type=F32)
    if act == "sigmoid":
        y = jax.nn.sigmoid(y)
    o_ref[...] = y.astype(o_ref.dtype)


def _norm_mm(x, g, w, layer, *, col0, ncols, tn, tm, act=None):
    m, k = x.shape
    j0 = col0 // tn
    assert col0 % tn == 0 and ncols % tn == 0
    w_spec = pl.BlockSpec((None, k, tn), lambda i, j: (layer, 0, j0 + j))
    return pl.pallas_call(
        functools.partial(_norm_mm_kernel, act=act),
        grid=(m // tm, ncols // tn),
        in_specs=[pl.BlockSpec((tm, k), lambda i, j: (i, 0)),
                  pl.BlockSpec((None, 1, k), lambda i, j: (layer, 0, 0)),
                  w_spec],
        out_specs=pl.BlockSpec((tm, tn), lambda i, j: (i, j)),
        out_shape=jax.ShapeDtypeStruct((m, ncols), F32),
        scratch_shapes=[pltpu.VMEM((tm, k), BF16)],
        compiler_params=_cparams("parallel", "arbitrary"),
        name="norm_mm",
    )(x, g, w)


def _rwkv_prep_kernel(za_ref, aux_ref, mu_ref, wl_ref, w0_ref, a0_ref, kkw_ref, ka_ref, rk_ref,
                      ones_ref, r_ref, w_ref, k_ref, v_ref, nkk_ref, b_ref, gate_ref, bonus_ref,
                      *, seq_tiles, rows_per_seq):
    za = za_ref[...]
    tm = za.shape[0]
    row = lax.broadcasted_iota(jnp.int32, za.shape, 0)
    rolled = pltpu.roll(za, 1, 0)
    if seq_tiles is not None:
        first = jnp.where(pl.program_id(0) % seq_tiles == 0, 0.0, aux_ref[7:8, :])
        prev = jnp.where(row == 0, first, rolled)
    else:
        prev = jnp.where(row % rows_per_seq == 0, aux_ref[...], rolled)
    zs = za + mu_ref[...] * (prev - za)
    da = w0_ref.shape[1]
    r = zs[:, 0:da]
    k = zs[:, da:2 * da]
    v = zs[:, 2 * da:3 * da]
    lr = zs[:, 3 * da:]
    col = lax.broadcasted_iota(jnp.int32, lr.shape, 1)
    lact = jnp.where(col < W_LORA, jnp.tanh(lr),
                     jnp.where(col < W_LORA + A_LORA, lr, jax.nn.sigmoid(lr)))
    lo = jnp.dot(lact.astype(BF16), wl_ref[...], preferred_element_type=F32)
    w_log = -jax.nn.softplus(-(w0_ref[...] + lo[:, 0:da])) - 0.5
    decay = jnp.exp(-jnp.exp(w_log))
    a = jax.nn.sigmoid(a0_ref[...] + lo[:, da:2 * da])
    gate = lo[:, 2 * da:3 * da]
    ones_bd = ones_ref[...]
    kk = k * kkw_ref[...]
    kk = kk * lax.rsqrt(jnp.maximum(_seg_sum(kk * kk, ones_bd), 1e-24))
    k2 = k * (1.0 + (a - 1.0) * ka_ref[...])
    r_ref[...] = r
    w_ref[...] = decay
    k_ref[...] = k2
    v_ref[...] = v
    nkk_ref[...] = -kk
    b_ref[...] = kk * a
    gate_ref[...] = gate
    bonus_ref[...] = _seg_sum(r * k2 * rk_ref[...], ones_bd) * v


def _rwkv_prep(za_all, aux, vecs, wl, ones_bd, layer, *, row0, nrows, tm, seq_len, aux_is_carry):
    wa = za_all.shape[1]
    da = vecs["w0"].shape[-1]
    i0 = row0 // tm
    if aux_is_carry:
        seq_tiles = seq_len // tm
        c0 = row0 // 8
        aux_spec = pl.BlockSpec((8, wa), lambda i: (jnp.maximum(c0 + i * (tm // 8) - 1, 0), 0))
        rows_per_seq = None
    else:
        seq_tiles = None
        rows_per_seq = seq_len
        aux_spec = pl.BlockSpec((tm, wa), lambda i: (i, 0))

    def vec(n):
        return pl.BlockSpec((None, 1, n), lambda i: (layer, 0, 0))

    out_spec = pl.BlockSpec((tm, da), lambda i: (i, 0))
    out_sds = jax.ShapeDtypeStruct((nrows, da), F32)
    out_specs, out_shape = [out_spec] * 8, [out_sds] * 8
    return pl.pallas_call(
        functools.partial(_rwkv_prep_kernel, seq_tiles=seq_tiles, rows_per_seq=rows_per_seq),
        grid=(nrows // tm,),
        in_specs=[pl.BlockSpec((tm, wa), lambda i: (i0 + i, 0)), aux_spec, vec(wa),
                  pl.BlockSpec((None,) + wl.shape[1:], lambda i: (layer, 0, 0)),
                  vec(da), vec(da), vec(da), vec(da), vec(da),
                  pl.BlockSpec(ones_bd.shape, lambda i: (0, 0))],
        out_specs=out_specs,
        out_shape=out_shape,
        compiler_params=_cparams("parallel"),
        name="rwkv_prep",
    )(za_all, aux, vecs["mu"], wl, vecs["w0"], vecs["a0"], vecs["k_k"], vecs["k_a"], vecs["r_k"],
      ones_bd)


def _rwkv_scan_kernel(*refs, nb, zero_init):
    if zero_init:
        s0_ref = None
        (r_ref, w_ref, k_ref, v_ref, nkk_ref, b_ref, ones_ref, ones2_ref, eye_ref, sel_ref,
         y_ref, so_ref, st_ref) = refs
    else:
        (s0_ref, r_ref, w_ref, k_ref, v_ref, nkk_ref, b_ref, ones_ref, ones2_ref, eye_ref, sel_ref,
         y_ref, so_ref, st_ref) = refs
    ci = pl.program_id(1)
    tc = r_ref.shape[1]
    npair = r_ref.shape[2] // LANE

    @pl.when(ci == 0)
    def _():
        for bb in range(nb):
            for p in range(npair):
                if zero_init:
                    st_ref[bb * npair + p] = jnp.zeros((A_HEAD, LANE), F32)
                else:
                    st_ref[bb * npair + p] = jnp.concatenate(
                        [s0_ref[bb, 2 * p], s0_ref[bb, 2 * p + 1]], axis=1)

    ones_bd = ones_ref[...]
    ones2 = ones2_ref[...]
    eye = eye_ref[...]
    sel = sel_ref[...]

    def step(t):
        for bb in range(nb):
            for p in range(npair):
                q = bb * npair + p
                idx = (bb, pl.ds(t, 1), pl.ds(p * LANE, LANE))
                s_old = st_ref[q]
                m1 = (s_old * nkk_ref[idx]).astype(BF16)
                sa = jnp.dot(m1, ones_bd, preferred_element_type=F32)
                v_row = v_ref[idx]
                v_hi = v_row.astype(BF16)
                v_lo = (v_row - v_hi.astype(F32)).astype(BF16)
                d = jnp.concatenate([eye * v_hi, eye * v_lo], axis=1)
                v_bc = jnp.dot(d, ones2, preferred_element_type=F32)
                s_new = s_old * w_ref[idx] + sa * b_ref[idx] + v_bc * k_ref[idx]
                st_ref[q] = s_new
                m2 = (s_new * r_ref[idx]).astype(BF16)
                y16 = lax.dot_general(sel, m2, (((1,), (1,)), ((), ())),
                                      preferred_element_type=F32)
                y_ref[idx] = jnp.concatenate([y16[0:1, :], y16[1:2, :]], axis=1)

    for t in range(tc):
        step(t)

    @pl.when(ci == pl.num_programs(1) - 1)
    def _():
        for bb in range(nb):
            for p in range(npair):
                s = st_ref[bb * npair + p]
                so_ref[bb, 2 * p] = s[:, :A_HEAD]
                so_ref[bb, 2 * p + 1] = s[:, A_HEAD:]


def _rwkv_scan(s0, seqs, consts, layer, *, nb, tc):
    bsz, t, da = seqs[0].shape
    seq_spec = pl.BlockSpec((nb, tc, da), lambda g, c: (g, c, 0))
    heads = da // A_HEAD
    zero_init = s0 is None
    st_spec = pl.BlockSpec((nb, heads, A_HEAD, A_HEAD), lambda g, c: (g, 0, 0, 0))
    const_specs = [pl.BlockSpec(c.shape, lambda g, c_: (0, 0)) for c in consts]
    s0_spec = pl.BlockSpec((None, nb, heads, A_HEAD, A_HEAD), lambda g, c: (layer, g, 0, 0, 0))
    in_specs = ([] if zero_init else [s0_spec]) + [seq_spec] * 6 + const_specs
    args = ([] if zero_init else [s0]) + list(seqs) + list(consts)
    return pl.pallas_call(
        functools.partial(_rwkv_scan_kernel, nb=nb, zero_init=zero_init),
        grid=(bsz // nb, t // tc),
        in_specs=in_specs,
        out_specs=[seq_spec, st_spec],
        out_shape=[jax.ShapeDtypeStruct(seqs[0].shape, F32),
                   jax.ShapeDtypeStruct((bsz, heads, A_HEAD, A_HEAD), F32)],
        scratch_shapes=[pltpu.VMEM((nb * da // LANE, A_HEAD, LANE), F32)],
        compiler_params=_cparams("parallel", "arbitrary"),
        name="rwkv_scan",
    )(*args)


def _rwkv_post_kernel(y_ref, bonus_ref, gate_ref, g_ref, b_ref, ones_ref, o_ref):
    y = y_ref[...]
    ones_bd = ones_ref[...]
    yc = y - _seg_sum(y, ones_bd) * (1.0 / A_HEAD)
    var = _seg_sum(yc * yc, ones_bd) * (1.0 / A_HEAD)
    yn = yc * lax.rsqrt(var + LNX_EPS) * g_ref[...] + b_ref[...]
    o_ref[...] = ((yn + bonus_ref[...]) * gate_ref[...]).astype(o_ref.dtype)


def _rwkv_post(y, bonus, gate, lnx_g, lnx_b, ones_bd, layer, *, tm):
    m, da = bonus.shape
    row = pl.BlockSpec((tm, da), lambda i: (i, 0))
    y_spec = row
    vec = pl.BlockSpec((None, 1, da), lambda i: (layer, 0, 0))
    return pl.pallas_call(
        _rwkv_post_kernel,
        grid=(m // tm,),
        in_specs=[y_spec, row, row, vec, vec, pl.BlockSpec(ones_bd.shape, lambda i: (0, 0))],
        out_specs=row,
        out_shape=jax.ShapeDtypeStruct((m, da), BF16),
        compiler_params=_cparams("parallel"),
        name="rwkv_post",
    )(y, bonus, gate, lnx_g, lnx_b, ones_bd)


def _mla_proj_kernel(zb_ref, cs_ref, sn_ref, qn_ref, kvn_ref, wn_ref, wr_ref, wrr_ref, wuk_ref,
                     q_ref, kall_ref, ckv_ref, kr_ref, *, lora, rope):
    zb = zb_ref[...]
    cs = cs_ref[...]
    sn = sn_ref[...]
    cq = _rms(zb[:, 0:lora], qn_ref[...]).astype(BF16)
    ckv = _rms(zb[:, lora:2 * lora], kvn_ref[...])
    kr = zb[:, 2 * lora:2 * lora + rope] * cs + zb[:, 2 * lora + LANE:2 * lora + LANE + rope] * sn
    ckv_ref[...] = ckv
    kr_ref[...] = kr
    pad = jnp.zeros((zb.shape[0], LANE - rope), BF16)
    kall_ref[:, 0:lora] = ckv.astype(BF16)
    kall_ref[:, lora:lora + LANE] = jnp.concatenate([kr.astype(BF16), pad], axis=1)
    heads = wuk_ref.shape[0]
    nope = wuk_ref.shape[1]
    qn = jnp.dot(cq, wn_ref[...].astype(BF16), preferred_element_type=F32)
    for h in range(heads):
        q_lat = _bdot(qn[:, h * nope:(h + 1) * nope], wuk_ref[h])
        q_rope = (jnp.dot(cq, wr_ref[h].astype(BF16), preferred_element_type=F32) * cs
                  + jnp.dot(cq, wrr_ref[h].astype(BF16), preferred_element_type=F32) * sn)
        q_ref[h, :, 0:lora] = q_lat.astype(BF16)
        q_ref[h, :, lora:lora + LANE] = jnp.concatenate([q_rope.astype(BF16), pad], axis=1)


def _mla_proj(zb, cs, sn, qn, kvn, wn, wr, wrr, wuk, layer, *, tm, lora, rope):
    m = zb.shape[0]
    heads, nope = wuk.shape[1], wuk.shape[2]
    dq = lora + LANE

    def full(a):
        nd = a.ndim - 1
        return pl.BlockSpec((None,) + a.shape[1:], lambda i: (layer,) + (0,) * nd)

    return pl.pallas_call(
        functools.partial(_mla_proj_kernel, lora=lora, rope=rope),
        grid=(m // tm,),
        in_specs=[pl.BlockSpec((tm, zb.shape[1]), lambda i: (i, 0)),
                  pl.BlockSpec((tm, rope), lambda i: (i, 0)),
                  pl.BlockSpec((tm, rope), lambda i: (i, 0)),
                  full(qn), full(kvn), full(wn), full(wr), full(wrr), full(wuk)],
        out_specs=[pl.BlockSpec((heads, tm, dq), lambda i: (0, i, 0)),
                   pl.BlockSpec((tm, dq), lambda i: (i, 0)),
                   pl.BlockSpec((tm, lora), lambda i: (i, 0)),
                   pl.BlockSpec((tm, rope), lambda i: (i, 0))],
        out_shape=[jax.ShapeDtypeStruct((heads, m, dq), BF16),
                   jax.ShapeDtypeStruct((m, dq), BF16),
                   jax.ShapeDtypeStruct((m, lora), F32),
                   jax.ShapeDtypeStruct((m, rope), F32)],
        compiler_params=_cparams("parallel"),
        name="mla_proj",
    )(zb, cs, sn, qn, kvn, wn, wr, wrr, wuk)


NEG_BIG = -1e30


def _flash_kernel(q_ref, k_ref, wuv_ref, o_ref, m_ref, l_ref, acc_ref, *, tq, tk, lora, scale):
    qi = pl.program_id(1)
    ki = pl.program_id(2)
    heads = q_ref.shape[0]

    @pl.when(ki == 0)
    def _():
        m_ref[...] = jnp.full(m_ref.shape, NEG_BIG, F32)
        l_ref[...] = jnp.zeros(l_ref.shape, F32)
        acc_ref[...] = jnp.zeros(acc_ref.shape, F32)

    @pl.when(ki * tk <= qi * tq + tq - 1)
    def _():
        q = q_ref[...].reshape(heads * tq, q_ref.shape[2])
        k = k_ref[...]
        s = lax.dot_general(q, k, (((1,), (1,)), ((), ())), preferred_element_type=F32) * scale
        qpos = qi * tq + lax.broadcasted_iota(jnp.int32, s.shape, 0) % tq
        kpos = ki * tk + lax.broadcasted_iota(jnp.int32, s.shape, 1)
        s = jnp.where(kpos <= qpos, s, NEG_BIG)
        m_prev = m_ref[...]
        m_new = jnp.maximum(m_prev, jnp.max(s, axis=-1, keepdims=True))
        alpha = jnp.exp(m_prev - m_new)
        p = jnp.exp(s - m_new)
        l_ref[...] = alpha * l_ref[...] + jnp.sum(p, axis=-1, keepdims=True)
        acc_ref[...] = alpha * acc_ref[...] + jnp.dot(p.astype(BF16), k[:, 0:lora],
                                                      preferred_element_type=F32)
        m_ref[...] = m_new

    @pl.when(ki == pl.num_programs(2) - 1)
    def _():
        o = (acc_ref[...] / l_ref[...]).astype(BF16)
        vdim = wuv_ref.shape[2]
        for h in range(heads):
            o_ref[:, h * vdim:(h + 1) * vdim] = jnp.dot(
                o[h * tq:(h + 1) * tq], wuv_ref[h].astype(BF16),
                preferred_element_type=F32).astype(o_ref.dtype)


def _flash_prompt(q, kall, wuv, layer, *, bsz, t, tq, tk, lora, scale):
    heads, _, dq = q.shape
    vdim = wuv.shape[3]
    nq, nk = t // tq, t // tk

    def k_map(b, i, j):
        return (b * nk + jnp.minimum(j, (i * tq + tq - 1) // tk), 0)

    return pl.pallas_call(
        functools.partial(_flash_kernel, tq=tq, tk=tk, lora=lora, scale=scale),
        grid=(bsz, nq, nk),
        in_specs=[pl.BlockSpec((heads, tq, dq), lambda b, i, j: (0, b * nq + i, 0)),
                  pl.BlockSpec((tk, dq), k_map),
                  pl.BlockSpec((None,) + wuv.shape[1:], lambda b, i, j: (layer, 0, 0, 0))],
        out_specs=pl.BlockSpec((tq, heads * vdim), lambda b, i, j: (b * nq + i, 0)),
        out_shape=jax.ShapeDtypeStruct((bsz * t, heads * vdim), BF16),
        scratch_shapes=[pltpu.VMEM((heads * tq, 1), F32), pltpu.VMEM((heads * tq, 1), F32),
                        pltpu.VMEM((heads * tq, lora), F32)],
        compiler_params=_cparams("parallel", "parallel", "arbitrary"),
        name="flash_prompt",
    )(q, kall, wuv)


def _paged_kernel(pt_ref, q_ref, *refs, npg, lora, rope, n_new, scale):
    ckv_refs = refs[:npg]
    kr_refs = refs[npg:2 * npg]
    knew_ref, o_ref, kbuf_ref, m_ref, l_ref, acc_ref = refs[2 * npg:]
    j = pl.program_id(1)
    q = q_ref[0]

    @pl.when(j == 0)
    def _():
        m_ref[...] = jnp.full(m_ref.shape, NEG_BIG, F32)
        l_ref[...] = jnp.zeros(l_ref.shape, F32)
        acc_ref[...] = jnp.zeros(acc_ref.shape, F32)
        kbuf_ref[:, lora:lora + LANE] = jnp.zeros((kbuf_ref.shape[0], LANE), BF16)

    for mpg in range(npg):
        rows = pl.ds(mpg * PAGE_SIZE, PAGE_SIZE)
        kbuf_ref[rows, 0:lora] = ckv_refs[mpg][...].astype(BF16)
        kbuf_ref[rows, lora:lora + rope] = kr_refs[mpg][...].astype(BF16)

    def update(s, vals):
        m_prev = m_ref[...]
        m_new = jnp.maximum(m_prev, jnp.max(s, axis=-1, keepdims=True))
        alpha = jnp.exp(m_prev - m_new)
        p = jnp.exp(s - m_new)
        l_ref[...] = alpha * l_ref[...] + jnp.sum(p, axis=-1, keepdims=True)
        acc_ref[...] = alpha * acc_ref[...] + jnp.dot(p.astype(BF16), vals,
                                                      preferred_element_type=F32)
        m_ref[...] = m_new

    kb = kbuf_ref[...]
    s = lax.dot_general(q, kb, (((1,), (1,)), ((), ())), preferred_element_type=F32) * scale
    update(s, kb[:, 0:lora])

    @pl.when(j == pl.num_programs(1) - 1)
    def _():
        kn = knew_ref[0]
        sn = lax.dot_general(q, kn, (((1,), (1,)), ((), ())), preferred_element_type=F32) * scale
        t_q = lax.broadcasted_iota(jnp.int32, sn.shape, 0) % n_new
        t_k = lax.broadcasted_iota(jnp.int32, sn.shape, 1)
        sn = jnp.where(t_k <= t_q, sn, NEG_BIG)
        update(sn, kn[:, 0:lora])
        o_ref[0] = acc_ref[...] / l_ref[...]


def _paged_attend(page_table, q, cache_ckv, cache_kr, knew, layer, *, npg, lora, rope, n_new, scale):
    bsz, rows, dq = q.shape
    n_pages = page_table.shape[1]
    assert n_pages % npg == 0

    def page_spec(width, mpg):
        return pl.BlockSpec((None, None, PAGE_SIZE, width),
                            lambda b, j, pt: (layer, pt[b, j * npg + mpg], 0, 0))

    in_specs = ([pl.BlockSpec((1, rows, dq), lambda b, j, pt: (b, 0, 0))]
                + [page_spec(lora, mpg) for mpg in range(npg)]
                + [page_spec(rope, mpg) for mpg in range(npg)]
                + [pl.BlockSpec((1,) + knew.shape[1:], lambda b, j, pt: (b, 0, 0))])
    grid_spec = pltpu.PrefetchScalarGridSpec(
        num_scalar_prefetch=1,
        grid=(bsz, n_pages // npg),
        in_specs=in_specs,
        out_specs=pl.BlockSpec((1, rows, lora), lambda b, j, pt: (b, 0, 0)),
        scratch_shapes=[pltpu.VMEM((npg * PAGE_SIZE, dq), BF16),
                        pltpu.VMEM((rows, 1), F32), pltpu.VMEM((rows, 1), F32),
                        pltpu.VMEM((rows, lora), F32)],
    )
    return pl.pallas_call(
        functools.partial(_paged_kernel, npg=npg, lora=lora, rope=rope, n_new=n_new, scale=scale),
        grid_spec=grid_spec,
        out_shape=jax.ShapeDtypeStruct((bsz, rows, lora), F32),
        compiler_params=_cparams("parallel", "arbitrary"),
        name="paged_attend",
    )(page_table, q, *([cache_ckv] * npg), *([cache_kr] * npg), knew)


def _uv_kernel(o_ref, wuv_ref, out_ref, *, n_new):
    heads, _, vdim = wuv_ref.shape
    nb = o_ref.shape[0]
    for h in range(heads):
        x = o_ref[:, h * n_new:(h + 1) * n_new, :].reshape(nb * n_new, o_ref.shape[2])
        out_ref[:, h * vdim:(h + 1) * vdim] = _bdot(x, wuv_ref[h]).astype(out_ref.dtype)


def _uv_sample(o_lat, wuv, layer, *, nb, n_new):
    bsz, rows, lora = o_lat.shape
    heads, vdim = wuv.shape[1], wuv.shape[3]
    return pl.pallas_call(
        functools.partial(_uv_kernel, n_new=n_new),
        grid=(bsz // nb,),
        in_specs=[pl.BlockSpec((nb, rows, lora), lambda i: (i, 0, 0)),
                  pl.BlockSpec((None,) + wuv.shape[1:], lambda i: (layer, 0, 0, 0))],
        out_specs=pl.BlockSpec((nb * n_new, heads * vdim), lambda i: (i, 0)),
        out_shape=jax.ShapeDtypeStruct((bsz * n_new, heads * vdim), BF16),
        compiler_params=_cparams("parallel"),
        name="uv_sample",
    )(o_lat, wuv)


def _gmlp_kernel(zc_ref, g_ref, b_ref, ws_ref, mask_ref, bias_ref, c_ref, v_ref):
    zc = zc_ref[...]
    zc = 0.5 * zc * (1.0 + lax.erf(zc * np.float32(np.sqrt(0.5))))
    dc = zc.shape[1] // 2
    u = zc[:, 0:dc]
    v = zc[:, dc:]
    vc = v - jnp.mean(v, axis=-1, keepdims=True)
    vn = vc * lax.rsqrt(jnp.mean(vc * vc, axis=-1, keepdims=True) + LN_EPS) * g_ref[...] + b_ref[...]
    v_ref[...] = vn
    mask = mask_ref[...]
    vb = vn.astype(BF16)
    groups = ws_ref.shape[0]
    gd = dc // groups
    for g in range(groups):
        cols = slice(g * gd, (g + 1) * gd)
        wm = jnp.where(mask > 0, ws_ref[g], 0.0).astype(BF16)
        for c in range(zc.shape[0] // CHUNK):
            rows = slice(c * CHUNK, (c + 1) * CHUNK)
            s = jnp.dot(wm, vb[rows, cols], preferred_element_type=F32) + bias_ref[:, cols]
            c_ref[rows, cols] = (u[rows, cols] * s).astype(c_ref.dtype)


def _gmlp(zc, ln_g, ln_b, ws, mask, bias, layer, *, tm, prompt_tiles):
    m, d2 = zc.shape
    dc = d2 // 2
    groups = ws.shape[2]

    def grp(i):
        return jnp.where(i < prompt_tiles, 0, 1)

    vec = pl.BlockSpec((None, 1, dc), lambda i: (layer, 0, 0))
    return pl.pallas_call(
        _gmlp_kernel,
        grid=(m // tm,),
        in_specs=[pl.BlockSpec((tm, d2), lambda i: (i, 0)), vec, vec,
                  pl.BlockSpec((None, None, groups, CHUNK, CHUNK), lambda i: (layer, grp(i), 0, 0, 0)),
                  pl.BlockSpec((None, CHUNK, CHUNK), lambda i: (grp(i), 0, 0)),
                  pl.BlockSpec((None, None, CHUNK, dc), lambda i: (layer, grp(i), 0, 0))],
        out_specs=[pl.BlockSpec((tm, dc), lambda i: (i, 0)), pl.BlockSpec((tm, dc), lambda i: (i, 0))],
        out_shape=[jax.ShapeDtypeStruct((m, dc), BF16), jax.ShapeDtypeStruct((m, dc), F32)],
        compiler_params=_cparams("parallel"),
        name="gmlp",
    )(zc, ln_g, ln_b, ws, mask, bias)


def _merge_kernel(a_ref, b_ref, c_ref, ga_ref, gb_ref, gc_ref, wa_ref, wb_ref, wc_ref, o_ref):
    m = (ga_ref[...] * jnp.dot(a_ref[...], wa_ref[...].astype(BF16), preferred_element_type=F32)
         + gb_ref[...] * jnp.dot(b_ref[...], wb_ref[...].astype(BF16), preferred_element_type=F32)
         + gc_ref[...] * jnp.dot(c_ref[...], wc_ref[...].astype(BF16), preferred_element_type=F32))
    o_ref[...] = m.astype(o_ref.dtype)


def _merge(a, b, c, sg, wa, wb, wc, layer, *, tm, tn):
    m, kd = a.shape
    n = wa.shape[2]
    nj = n // tn
    pre = pl.BlockSpec((tm, kd), lambda i, j: (i, 0))

    def gate(s):
        return pl.BlockSpec((tm, tn), lambda i, j: (i, s * nj + j))

    w = pl.BlockSpec((None, kd, tn), lambda i, j: (layer, 0, j))
    return pl.pallas_call(
        _merge_kernel,
        grid=(m // tm, nj),
        in_specs=[pre, pre, pre, gate(0), gate(1), gate(2), w, w, w],
        out_specs=pl.BlockSpec((tm, tn), lambda i, j: (i, j)),
        out_shape=jax.ShapeDtypeStruct((m, n), BF16),
        compiler_params=_cparams("parallel", "arbitrary"),
        name="merge",
    )(a, b, c, sg, sg, sg, wa, wb, wc)


def _wo_kernel(m_ref, w_ref, x_ref, g_ref, o_ref, acc_ref):
    kk = pl.program_id(1)

    @pl.when(kk == 0)
    def _():
        acc_ref[...] = jnp.zeros(acc_ref.shape, F32)

    acc_ref[...] += jnp.dot(m_ref[...], w_ref[...].astype(BF16), preferred_element_type=F32)

    @pl.when(kk == pl.num_programs(1) - 1)
    def _():
        o_ref[...] = x_ref[...] + _rms(acc_ref[...], g_ref[...])


def _wo_norm(mm, w, x, g, layer, *, tm, tk):
    m, d = x.shape
    return pl.pallas_call(
        _wo_kernel,
        grid=(m // tm, d // tk),
        in_specs=[pl.BlockSpec((tm, tk), lambda i, k: (i, k)),
                  pl.BlockSpec((None, tk, d), lambda i, k: (layer, k, 0)),
                  pl.BlockSpec((tm, d), lambda i, k: (i, 0)),
                  pl.BlockSpec((None, 1, d), lambda i, k: (layer, 0, 0))],
        out_specs=pl.BlockSpec((tm, d), lambda i, k: (i, 0)),
        out_shape=jax.ShapeDtypeStruct((m, d), F32),
        scratch_shapes=[pltpu.VMEM((tm, d), F32)],
        compiler_params=_cparams("parallel", "arbitrary"),
        name="wo_norm",
    )(mm, w, x, g)


def _ffn_kernel(x_ref, gpre_ref, up_ref, down_ref, gpost_ref, o_ref, hn_ref, acc_ref):
    f = pl.program_id(1)

    @pl.when(f == 0)
    def _():
        hn_ref[...] = _rms(x_ref[...], gpre_ref[...]).astype(BF16)
        acc_ref[...] = jnp.zeros(acc_ref.shape, F32)

    a = jnp.dot(hn_ref[...], up_ref[...].astype(BF16), preferred_element_type=F32)
    a = jnp.square(jnp.maximum(a, 0.0))
    acc_ref[...] += jnp.dot(a.astype(BF16), down_ref[...].astype(BF16), preferred_element_type=F32)

    @pl.when(f == pl.num_programs(1) - 1)
    def _():
        o_ref[...] = x_ref[...] + _rms(acc_ref[...], gpost_ref[...])


def _ffn(x, gpre, up, down, gpost, layer, *, tm, tf):
    m, d = x.shape
    dff = up.shape[2]
    vec = pl.BlockSpec((None, 1, d), lambda i, f: (layer, 0, 0))
    return pl.pallas_call(
        _ffn_kernel,
        grid=(m // tm, dff // tf),
        in_specs=[pl.BlockSpec((tm, d), lambda i, f: (i, 0)), vec,
                  pl.BlockSpec((None, d, tf), lambda i, f: (layer, 0, f)),
                  pl.BlockSpec((None, tf, d), lambda i, f: (layer, f, 0)), vec],
        out_specs=pl.BlockSpec((tm, d), lambda i, f: (i, 0)),
        out_shape=jax.ShapeDtypeStruct((m, d), F32),
        scratch_shapes=[pltpu.VMEM((tm, d), BF16), pltpu.VMEM((tm, d), F32)],
        compiler_params=_cparams("parallel", "arbitrary"),
        name="ffn",
    )(x, gpre, up, down, gpost)


def _ple_kernel(x_ref, xc_ref, p_ref, wg_ref, wp_ref, o_ref, xb_ref):
    @pl.when(pl.program_id(1) == 0)
    def _():
        xb_ref[...] = x_ref[...].astype(BF16)

    gate = jax.nn.sigmoid(jnp.dot(xb_ref[...], wg_ref[...].astype(BF16), preferred_element_type=F32))
    proj = _bdot(p_ref[...], wp_ref[...])
    o_ref[...] = xc_ref[...] + gate * proj


def _ple(x, p, wg, wp, layer, *, tm, tn):
    m, d = x.shape
    pd = p.shape[1]
    return pl.pallas_call(
        _ple_kernel,
        grid=(m // tm, d // tn),
        in_specs=[pl.BlockSpec((tm, d), lambda i, j: (i, 0)),
                  pl.BlockSpec((tm, tn), lambda i, j: (i, j)),
                  pl.BlockSpec((tm, pd), lambda i, j: (i, 0)),
                  pl.BlockSpec((None, d, tn), lambda i, j: (layer, 0, j)),
                  pl.BlockSpec((None, pd, tn), lambda i, j: (layer, 0, j))],
        out_specs=pl.BlockSpec((tm, tn), lambda i, j: (i, j)),
        out_shape=jax.ShapeDtypeStruct((m, d), F32),
        scratch_shapes=[pltpu.VMEM((tm, d), BF16)],
        compiler_params=_cparams("parallel", "arbitrary"),
        name="ple",
    )(x, x, p, wg, wp)


def _scan_consts():
    lane = np.arange(LANE)
    ones_bd = (lane[:, None] // A_HEAD == lane[None, :] // A_HEAD).astype(np.float32)
    ones2 = np.concatenate([ones_bd, ones_bd], axis=0)
    eye = (lane[None, :] % A_HEAD == np.arange(A_HEAD)[:, None]).astype(np.float32)
    sel = np.zeros((16, LANE), np.float32)
    sel[0, :A_HEAD] = 1.0
    sel[1, A_HEAD:] = 1.0
    return [jnp.asarray(a, BF16) for a in (ones_bd, ones2, eye, sel)]


def _rope_tables(pos, rope):
    inv_freq = ROPE_THETA ** (-jnp.arange(0, rope, 2, dtype=F32) / rope)
    ang = pos.astype(F32)[:, None] * inv_freq[None, :]
    cos, sin = jnp.cos(ang), jnp.sin(ang)
    return jnp.concatenate([cos, cos], axis=-1), jnp.concatenate([sin, sin], axis=-1)


def _rot_half_cols(w):
    half = w.shape[-1] // 2
    return jnp.concatenate([-w[..., half:], w[..., :half]], axis=-1)


def kernel(x_prompt, x_sample, state_rwkv, state_rwkv_shift, cache_ckv, cache_kr, page_table,
           p_prompt, p_sample, norm_mix_pre, norm_mix_post, norm_ffn_pre, norm_ffn_post, w_in,
           rwkv_mu, rwkv_w0, rwkv_w2, rwkv_a0, rwkv_a2, rwkv_g2, rwkv_k_k, rwkv_k_a, rwkv_r_k,
           rwkv_lnx_g, rwkv_lnx_b, w_out_a, mla_q_norm, mla_w_uq, mla_kv_norm, mla_w_uk, mla_w_uv,
           w_out_b, gmlp_ln_g, gmlp_ln_b, gmlp_w_s, gmlp_b_s, w_out_c, w_o, ffn_up, ffn_down,
           ple_proj, ple_gate):
    depth = w_in.shape[0]
    bsz, seq, d = x_prompt.shape
    dbsz, dseq, _ = x_sample.shape
    mp, ms = bsz * seq, dbsz * dseq
    mt = mp + ms
    da = rwkv_w0.shape[1]
    heads_a = da // A_HEAD
    n_a_in = rwkv_mu.shape[1]
    lora = mla_q_norm.shape[1]
    kv_lora = mla_kv_norm.shape[1]
    assert lora == kv_lora
    b_heads, nope = mla_w_uk.shape[2], mla_w_uk.shape[3]
    rope = mla_w_uq.shape[3] - nope
    vdim = mla_w_uv.shape[3]
    dc = gmlp_ln_g.shape[1]
    groups = gmlp_w_s.shape[1]
    past = page_table.shape[1] * PAGE_SIZE
    scale = float((nope + rope) ** -0.5)
    n_b_in = lora + kv_lora + rope
    col_b = n_a_in
    col_c = n_a_in + n_b_in
    col_g = col_c + 2 * dc
    wa_ext = -(-n_a_in // 512) * 512
    assert 3 * da + 512 == wa_ext and col_c % 512 == 0 and col_g % 512 == 0

    def v3(a):
        return a.reshape(a.shape[0], 1, -1)

    kr0 = col_b + lora + kv_lora
    zpad = jnp.zeros((depth, d, LANE - rope), F32)
    w_b = jnp.concatenate([w_in[:, :, col_b:kr0], w_in[:, :, kr0:kr0 + rope], zpad,
                           _rot_half_cols(w_in[:, :, kr0:kr0 + rope]), zpad], axis=-1)
    wl = jnp.zeros((depth, wa_ext - 3 * da, 3 * da), F32)
    wl = wl.at[:, 0:W_LORA, 0:da].set(rwkv_w2)
    wl = wl.at[:, W_LORA:W_LORA + A_LORA, da:2 * da].set(rwkv_a2)
    wl = wl.at[:, W_LORA + A_LORA:W_LORA + A_LORA + G_LORA, 2 * da:3 * da].set(rwkv_g2)
    wl = wl.astype(BF16)
    mu_ext = jnp.pad(rwkv_mu, ((0, 0), (0, wa_ext - n_a_in)))
    rwkv_vecs = {"mu": v3(mu_ext), "w0": v3(rwkv_w0), "a0": v3(rwkv_a0), "k_k": v3(rwkv_k_k),
                 "k_a": v3(rwkv_k_a), "r_k": v3(rwkv_r_k)}
    w_qn = mla_w_uq[..., :nope].reshape(depth, lora, b_heads * nope)
    w_qr = jnp.transpose(mla_w_uq[..., nope:], (0, 2, 1, 3))
    w_qrr = _rot_half_cols(w_qr)
    w_ukt = jnp.transpose(mla_w_uk, (0, 2, 3, 1))
    w_uvt = jnp.transpose(mla_w_uv, (0, 2, 1, 3))
    seqs_per_chunk = CHUNK // dseq
    tri = np.tril(np.ones((CHUNK, CHUNK), np.float32))
    blk = np.kron(np.eye(seqs_per_chunk, dtype=np.float32), np.tril(np.ones((dseq, dseq), np.float32)))
    sp_mask = jnp.asarray(np.stack([tri, blk]))
    ws_s = jnp.tile(gmlp_w_s[:, :, :dseq, :dseq], (1, 1, seqs_per_chunk, seqs_per_chunk))
    ws_all = jnp.stack([gmlp_w_s, ws_s], axis=1)
    gd = dc // groups
    bias_p = jnp.repeat(jnp.swapaxes(gmlp_b_s, 1, 2), gd, axis=2)
    bias_s = jnp.tile(bias_p[:, :dseq], (1, seqs_per_chunk, 1))
    bias_all = jnp.stack([bias_p, bias_s], axis=1)

    cs_p, sn_p = _rope_tables(jnp.arange(seq), rope)
    cs_s, sn_s = _rope_tables(past + jnp.arange(dseq), rope)
    cs = jnp.concatenate([jnp.tile(cs_p, (bsz, 1)), jnp.tile(cs_s, (dbsz, 1))], axis=0)
    sn = jnp.concatenate([jnp.tile(sn_p, (bsz, 1)), jnp.tile(sn_s, (dbsz, 1))], axis=0)
    ones_bd, ones2, eye, sel = _scan_consts()

    x = jnp.concatenate([x_prompt.reshape(mp, d), x_sample.reshape(ms, d)], axis=0)
    ple_all = jnp.concatenate([p_prompt.reshape(depth, mp, -1), p_sample.reshape(depth, ms, -1)], axis=1)

    tm_big = _pick_tile(mt, 1024)
    tm_mid = _pick_tile(mt, 512)
    tm_prep_p = _pick_tile(seq, 256, 8)
    tm_prep_s = _pick_tile(ms, 256, 8)
    tm_g = _pick_tile(int(np.gcd(mp, ms)), 512)
    tq = _pick_tile(seq, 128)
    tk = _pick_tile(seq, 512)
    nb_s = 2 if dbsz % 2 == 0 else 1
    npg = 8 if page_table.shape[1] % 8 == 0 else 1
    nb_uv = _pick_tile(dbsz, 16, 1)

    outs = {k: [] for k in ("st_p", "sh_p", "ckv_p", "kr_p", "st_s", "sh_s", "ckv_s", "kr_s", "vc_s")}
    for i in range(depth):
        g_pre = v3(norm_mix_pre)
        za = _norm_mm(x, g_pre, w_in, i, col0=0, ncols=wa_ext, tn=512, tm=tm_big)
        zb = _norm_mm(x, g_pre, w_b, i, col0=0, ncols=w_b.shape[2], tn=w_b.shape[2] // 2, tm=tm_big)
        zc = _norm_mm(x, g_pre, w_in, i, col0=col_c, ncols=2 * dc, tn=512, tm=tm_big)
        sg = _norm_mm(x, g_pre, w_in, i, col0=col_g, ncols=3 * d, tn=512, tm=tm_big, act="sigmoid")

        shift0 = jnp.pad(state_rwkv_shift[i], ((0, 0), (0, wa_ext - n_a_in)))
        shift_rows = jnp.repeat(shift0, dseq, axis=0)
        prep_p = _rwkv_prep(za, za, rwkv_vecs, wl, ones_bd, i, row0=0, nrows=mp, tm=tm_prep_p,
                            seq_len=seq, aux_is_carry=True)
        prep_s = _rwkv_prep(za, shift_rows, rwkv_vecs, wl, ones_bd, i, row0=mp, nrows=ms,
                            tm=tm_prep_s, seq_len=dseq, aux_is_carry=False)
        consts = [ones_bd, ones2, eye, sel]
        y_p, st_p = _rwkv_scan(None, [a.reshape(bsz, seq, da) for a in prep_p[:6]], consts, i,
                               nb=bsz, tc=8)
        y_s, st_s = _rwkv_scan(state_rwkv, [a.reshape(dbsz, dseq, da) for a in prep_s[:6]],
                               consts, i, nb=nb_s, tc=dseq)
        a_p = _rwkv_post(y_p.reshape(mp, da), prep_p[7], prep_p[6], v3(rwkv_lnx_g), v3(rwkv_lnx_b),
                         ones_bd, i, tm=_pick_tile(mp, 512))
        a_s = _rwkv_post(y_s.reshape(ms, da), prep_s[7], prep_s[6], v3(rwkv_lnx_g), v3(rwkv_lnx_b),
                         ones_bd, i, tm=_pick_tile(ms, 512))
        a_pre = jnp.concatenate([a_p, a_s], axis=0)
        sh_p = za[:mp].reshape(bsz, seq, wa_ext)[:, -1, :n_a_in]
        sh_s = za[mp:].reshape(dbsz, dseq, wa_ext)[:, -1, :n_a_in]

        q_all, k_all, ckv, kr = _mla_proj(zb, cs, sn, v3(mla_q_norm), v3(mla_kv_norm), w_qn, w_qr,
                                          w_qrr, w_ukt, i, tm=tm_mid, lora=lora, rope=rope)
        b_p = _flash_prompt(q_all, k_all, w_uvt, i, bsz=bsz, t=seq, tq=tq, tk=tk, lora=lora,
                            scale=scale)
        q_s = q_all[:, mp:].reshape(b_heads, dbsz, dseq, -1)
        q_s = jnp.transpose(q_s, (1, 0, 2, 3)).reshape(dbsz, b_heads * dseq, -1)
        knew = jnp.pad(k_all[mp:].reshape(dbsz, dseq, -1), ((0, 0), (0, 16 - dseq), (0, 0)))
        o_s = _paged_attend(page_table, q_s, cache_ckv, cache_kr, knew, i, npg=npg, lora=lora,
                            rope=rope, n_new=dseq, scale=scale)
        b_s = _uv_sample(o_s, w_uvt, i, nb=nb_uv, n_new=dseq)
        b_pre = jnp.concatenate([b_p, b_s], axis=0)

        c_pre, v_c = _gmlp(zc, v3(gmlp_ln_g), v3(gmlp_ln_b), ws_all, sp_mask, bias_all, i,
                           tm=tm_g, prompt_tiles=mp // tm_g)

        mm = _merge(a_pre, b_pre, c_pre, sg, w_out_a, w_out_b, w_out_c, i, tm=tm_big, tn=512)
        x = _wo_norm(mm, w_o, x, v3(norm_mix_post), i, tm=tm_mid, tk=512)
        x = _ffn(x, v3(norm_ffn_pre), ffn_up, ffn_down, v3(norm_ffn_post), i, tm=tm_mid, tf=512)
        x = _ple(x, ple_all[i], ple_gate, ple_proj, i, tm=tm_big, tn=512)

        outs["st_p"].append(st_p)
        outs["sh_p"].append(sh_p)
        outs["ckv_p"].append(ckv[:mp].reshape(bsz, seq, kv_lora))
        outs["kr_p"].append(kr[:mp].reshape(bsz, seq, rope))
        outs["st_s"].append(st_s)
        outs["sh_s"].append(sh_s)
        outs["ckv_s"].append(ckv[mp:].reshape(dbsz, dseq, kv_lora))
        outs["kr_s"].append(kr[mp:].reshape(dbsz, dseq, rope))
        outs["vc_s"].append(v_c[mp:].reshape(dbsz, dseq, dc))

    return (x[:mp].reshape(bsz, seq, d), x[mp:].reshape(dbsz, dseq, d),
            jnp.stack(outs["st_p"]), jnp.stack(outs["sh_p"]), jnp.stack(outs["ckv_p"]),
            jnp.stack(outs["kr_p"]), jnp.stack(outs["st_s"]), jnp.stack(outs["sh_s"]),
            jnp.stack(outs["ckv_s"]), jnp.stack(outs["kr_s"]), jnp.stack(outs["vc_s"]))
```

```python
import functools

import jax
import jax.numpy as jnp
import numpy as np
from jax import lax
from jax.experimental import pallas as pl
from jax.experimental.pallas import tpu as pltpu

F32 = jnp.float32
BF16 = jnp.bfloat16

NORM_EPS = 1e-6
LNX_EPS = 64e-5
LN_EPS = 1e-5
ROPE_THETA = 10000.0
PAGE_SIZE = 128
CHUNK = 128
A_HEAD = 64
W_LORA = 96
A_LORA = 96
G_LORA = 256
LANE = 128
VMEM_LIMIT = 56 * 1024 * 1024


def _cparams(*sem):
    return pltpu.CompilerParams(dimension_semantics=sem, vmem_limit_bytes=VMEM_LIMIT)


def _pick_tile(m, target, quantum=LANE):
    t = min(target, m)
    t -= t % quantum
    while m % t:
        t -= quantum
    return t


def _rms(x, g):
    return x * lax.rsqrt(jnp.mean(x * x, axis=-1, keepdims=True) + NORM_EPS) * g


def _bdot(a, b):
    return jnp.dot(a.astype(BF16), b.astype(BF16), preferred_element_type=F32)


def _seg_sum(x, ones_bd):
    outs = []
    for c in range(x.shape[1] // LANE):
        xb = x[:, c * LANE:(c + 1) * LANE]
        hi = xb.astype(BF16)
        r1 = xb - hi.astype(F32)
        mid = r1.astype(BF16)
        lo = (r1 - mid.astype(F32)).astype(BF16)
        s = (jnp.dot(hi, ones_bd, preferred_element_type=F32)
             + jnp.dot(mid, ones_bd, preferred_element_type=F32)
             + jnp.dot(lo, ones_bd, preferred_element_type=F32))
        outs.append(s)
    return jnp.concatenate(outs, axis=1)


def _norm_mm_kernel(x_ref, g_ref, w_ref, o_ref, xn_ref, *, act):
    @pl.when(pl.program_id(1) == 0)
    def _():
        xn_ref[...] = _rms(x_ref[...], g_ref[...]).astype(BF16)

    y = jnp.dot(xn_ref[...], w_ref[...].astype(BF16), preferred_element_type=F32)
    if act == "sigmoid":
        y = jax.nn.sigmoid(y)
    o_ref[...] = y.astype(o_ref.dtype)


def _norm_mm(x, g, w, layer, *, col0, ncols, tn, tm, act=None):
    m, k = x.shape
    j0 = col0 // tn
    assert col0 % tn == 0 and ncols % tn == 0
    w_spec = pl.BlockSpec((None, k, tn), lambda i, j: (layer, 0, j0 + j))
    return pl.pallas_call(
        functools.partial(_norm_mm_kernel, act=act),
        grid=(m // tm, ncols // tn),
        in_specs=[pl.BlockSpec((tm, k), lambda i, j: (i, 0)),
                  pl.BlockSpec((None, 1, k), lambda i, j: (layer, 0, 0)),
                  w_spec],
        out_specs=pl.BlockSpec((tm, tn), lambda i, j: (i, j)),
        out_shape=jax.ShapeDtypeStruct((m, ncols), F32),
        scratch_shapes=[pltpu.VMEM((tm, k), BF16)],
        compiler_params=_cparams("parallel", "arbitrary"),
        name="norm_mm",
    )(x, g, w)


def _rwkv_prep_kernel(za_ref, aux_ref, mu_ref, wl_ref, w0_ref, a0_ref, kkw_ref, ka_ref, rk_ref,
                      ones_ref, r_ref, w_ref, k_ref, v_ref, nkk_ref, b_ref, gate_ref, bonus_ref,
                      *, seq_tiles, rows_per_seq):
    za = za_ref[...]
    tm = za.shape[0]
    row = lax.broadcasted_iota(jnp.int32, za.shape, 0)
    rolled = pltpu.roll(za, 1, 0)
    if seq_tiles is not None:
        first = jnp.where(pl.program_id(0) % seq_tiles == 0, 0.0, aux_ref[7:8, :])
        prev = jnp.where(row == 0, first, rolled)
    else:
        prev = jnp.where(row % rows_per_seq == 0, aux_ref[...], rolled)
    zs = za + mu_ref[...] * (prev - za)
    da = w0_ref.shape[1]
    r = zs[:, 0:da]
    k = zs[:, da:2 * da]
    v = zs[:, 2 * da:3 * da]
    lr = zs[:, 3 * da:]
    col = lax.broadcasted_iota(jnp.int32, lr.shape, 1)
    lact = jnp.where(col < W_LORA, jnp.tanh(lr),
                     jnp.where(col < W_LORA + A_LORA, lr, jax.nn.sigmoid(lr)))
    lo = jnp.dot(lact.astype(BF16), wl_ref[...], preferred_element_type=F32)
    w_log = -jax.nn.softplus(-(w0_ref[...] + lo[:, 0:da])) - 0.5
    decay = jnp.exp(-jnp.exp(w_log))
    a = jax.nn.sigmoid(a0_ref[...] + lo[:, da:2 * da])
    gate = lo[:, 2 * da:3 * da]
    ones_bd = ones_ref[...]
    kk = k * kkw_ref[...]
    kk = kk * lax.rsqrt(jnp.maximum(_seg_sum(kk * kk, ones_bd), 1e-24))
    k2 = k * (1.0 + (a - 1.0) * ka_ref[...])
    r_ref[...] = r
    w_ref[...] = decay
    k_ref[...] = k2
    v_ref[...] = v
    nkk_ref[...] = -kk
    b_ref[...] = kk * a
    gate_ref[...] = gate
    bonus_ref[...] = _seg_sum(r * k2 * rk_ref[...], ones_bd) * v


def _rwkv_prep(za_all, aux, vecs, wl, ones_bd, layer, *, row0, nrows, tm, seq_len, aux_is_carry):
    wa = za_all.shape[1]
    da = vecs["w0"].shape[-1]
    i0 = row0 // tm
    if aux_is_carry:
        seq_tiles = seq_len // tm
        c0 = row0 // 8
        aux_spec = pl.BlockSpec((8, wa), lambda i: (jnp.maximum(c0 + i * (tm // 8) - 1, 0), 0))
        rows_per_seq = None
    else:
        seq_tiles = None
        rows_per_seq = seq_len
        aux_spec = pl.BlockSpec((tm, wa), lambda i: (i, 0))

    def vec(n):
        return pl.BlockSpec((None, 1, n), lambda i: (layer, 0, 0))

    out_spec = pl.BlockSpec((tm, da), lambda i: (i, 0))
    out_sds = jax.ShapeDtypeStruct((nrows, da), F32)
    out_specs, out_shape = [out_spec] * 8, [out_sds] * 8
    return pl.pallas_call(
        functools.partial(_rwkv_prep_kernel, seq_tiles=seq_tiles, rows_per_seq=rows_per_seq),
        grid=(nrows // tm,),
        in_specs=[pl.BlockSpec((tm, wa), lambda i: (i0 + i, 0)), aux_spec, vec(wa),
                  pl.BlockSpec((None,) + wl.shape[1:], lambda i: (layer, 0, 0)),
                  vec(da), vec(da), vec(da), vec(da), vec(da),
                  pl.BlockSpec(ones_bd.shape, lambda i: (0, 0))],
        out_specs=out_specs,
        out_shape=out_shape,
        compiler_params=_cparams("parallel"),
        name="rwkv_prep",
    )(za_all, aux, vecs["mu"], wl, vecs["w0"], vecs["a0"], vecs["k_k"], vecs["k_a"], vecs["r_k"],
      ones_bd)


def _rwkv_scan_kernel(*refs, nb, zero_init):
    if zero_init:
        s0_ref = None
        (r_ref, w_ref, k_ref, v_ref, nkk_ref, b_ref, ones_ref, eye_ref, sel_ref,
         y_ref, so_ref, st_ref) = refs
    else:
        (s0_ref, r_ref, w_ref, k_ref, v_ref, nkk_ref, b_ref, ones_ref, eye_ref, sel_ref,
         y_ref, so_ref, st_ref) = refs
    ci = pl.program_id(1)
    tc = r_ref.shape[1]
    npair = r_ref.shape[2] // LANE
    nq = nb * npair

    @pl.when(ci == 0)
    def _():
        for bb in range(nb):
            for p in range(npair):
                rows = pl.ds((bb * npair + p) * A_HEAD, A_HEAD)
                if zero_init:
                    st_ref[rows, :] = jnp.zeros((A_HEAD, LANE), F32)
                else:
                    st_ref[rows, :] = jnp.concatenate(
                        [s0_ref[bb, 2 * p], s0_ref[bb, 2 * p + 1]], axis=1)

    ones_bd = ones_ref[...]
    eye = eye_ref[...]
    sel = sel_ref[...]
    low_half = lax.broadcasted_iota(jnp.int32, (1, LANE), 1) < A_HEAD

    def rows_of(ref, t):
        return jnp.concatenate(
            [jnp.broadcast_to(ref[bb, pl.ds(t, 1), pl.ds(p * LANE, LANE)], (A_HEAD, LANE))
             for bb in range(nb) for p in range(npair)], axis=0)

    for t in range(tc):
        s_old = st_ref[...]
        m1 = (s_old * rows_of(nkk_ref, t)).astype(BF16)
        sa = jnp.dot(m1, ones_bd, preferred_element_type=F32)
        d = (eye * rows_of(v_ref, t)).astype(BF16)
        v_bc = jnp.dot(d, ones_bd, preferred_element_type=F32)
        s_new = s_old * rows_of(w_ref, t) + sa * rows_of(b_ref, t) + v_bc * rows_of(k_ref, t)
        st_ref[...] = s_new
        m2 = (s_new * rows_of(r_ref, t)).astype(BF16)
        y16 = lax.dot_general(sel, m2, (((1,), (1,)), ((), ())), preferred_element_type=F32)
        for j in range(nq // 2):
            r0 = y16[0:1, j * LANE:(j + 1) * LANE]
            r1 = y16[1:2, j * LANE:(j + 1) * LANE]
            even = jnp.where(low_half, r0, pltpu.roll(r1, A_HEAD, 1))
            odd = jnp.where(low_half, pltpu.roll(r0, A_HEAD, 1), r1)
            for q, val in ((2 * j, even), (2 * j + 1, odd)):
                y_ref[q // npair, pl.ds(t, 1), pl.ds((q % npair) * LANE, LANE)] = val

    @pl.when(ci == pl.num_programs(1) - 1)
    def _():
        for bb in range(nb):
            for p in range(npair):
                s = st_ref[pl.ds((bb * npair + p) * A_HEAD, A_HEAD), :]
                so_ref[bb, 2 * p] = s[:, :A_HEAD]
                so_ref[bb, 2 * p + 1] = s[:, A_HEAD:]


def _rwkv_scan(s0, seqs, consts, layer, *, nb, tc):
    bsz, t, da = seqs[0].shape
    seq_spec = pl.BlockSpec((nb, tc, da), lambda g, c: (g, c, 0))
    heads = da // A_HEAD
    zero_init = s0 is None
    st_spec = pl.BlockSpec((nb, heads, A_HEAD, A_HEAD), lambda g, c: (g, 0, 0, 0))
    const_specs = [pl.BlockSpec(c.shape, lambda g, c_: (0, 0)) for c in consts]
    s0_spec = pl.BlockSpec((None, nb, heads, A_HEAD, A_HEAD), lambda g, c: (layer, g, 0, 0, 0))
    in_specs = ([] if zero_init else [s0_spec]) + [seq_spec] * 6 + const_specs
    args = ([] if zero_init else [s0]) + list(seqs) + list(consts)
    return pl.pallas_call(
        functools.partial(_rwkv_scan_kernel, nb=nb, zero_init=zero_init),
        grid=(bsz // nb, t // tc),
        in_specs=in_specs,
        out_specs=[seq_spec, st_spec],
        out_shape=[jax.ShapeDtypeStruct(seqs[0].shape, F32),
                   jax.ShapeDtypeStruct((bsz, heads, A_HEAD, A_HEAD), F32)],
        scratch_shapes=[pltpu.VMEM((nb * da // LANE * A_HEAD, LANE), F32)],
        compiler_params=_cparams("parallel", "arbitrary"),
        name="rwkv_scan",
    )(*args)


def _rwkv_post_kernel(y_ref, bonus_ref, gate_ref, g_ref, b_ref, ones_ref, o_ref):
    y = y_ref[...]
    ones_bd = ones_ref[...]
    yc = y - _seg_sum(y, ones_bd) * (1.0 / A_HEAD)
    var = _seg_sum(yc * yc, ones_bd) * (1.0 / A_HEAD)
    yn = yc * lax.rsqrt(var + LNX_EPS) * g_ref[...] + b_ref[...]
    o_ref[...] = ((yn + bonus_ref[...]) * gate_ref[...]).astype(o_ref.dtype)


def _rwkv_post(y, bonus, gate, lnx_g, lnx_b, ones_bd, layer, *, tm):
    m, da = bonus.shape
    row = pl.BlockSpec((tm, da), lambda i: (i, 0))
    y_spec = row
    vec = pl.BlockSpec((None, 1, da), lambda i: (layer, 0, 0))
    return pl.pallas_call(
        _rwkv_post_kernel,
        grid=(m // tm,),
        in_specs=[y_spec, row, row, vec, vec, pl.BlockSpec(ones_bd.shape, lambda i: (0, 0))],
        out_specs=row,
        out_shape=jax.ShapeDtypeStruct((m, da), BF16),
        compiler_params=_cparams("parallel"),
        name="rwkv_post",
    )(y, bonus, gate, lnx_g, lnx_b, ones_bd)


def _mla_proj_kernel(zb_ref, cs_ref, sn_ref, qn_ref, kvn_ref, wn_ref, wr_ref, wrr_ref, wuk_ref,
                     q_ref, kall_ref, ckv_ref, kr_ref, *, lora, rope):
    zb = zb_ref[...]
    cs = cs_ref[...]
    sn = sn_ref[...]
    cq = _rms(zb[:, 0:lora], qn_ref[...]).astype(BF16)
    ckv = _rms(zb[:, lora:2 * lora], kvn_ref[...])
    kr = zb[:, 2 * lora:2 * lora + rope] * cs + zb[:, 2 * lora + LANE:2 * lora + LANE + rope] * sn
    ckv_ref[...] = ckv
    kr_ref[...] = kr
    pad = jnp.zeros((zb.shape[0], LANE - rope), BF16)
    kall_ref[:, 0:lora] = ckv.astype(BF16)
    kall_ref[:, lora:lora + LANE] = jnp.concatenate([kr.astype(BF16), pad], axis=1)
    heads = wuk_ref.shape[0]
    nope = wuk_ref.shape[1]
    qn = jnp.dot(cq, wn_ref[...].astype(BF16), preferred_element_type=F32)
    for h in range(heads):
        q_lat = _bdot(qn[:, h * nope:(h + 1) * nope], wuk_ref[h])
        q_rope = (jnp.dot(cq, wr_ref[h].astype(BF16), preferred_element_type=F32) * cs
                  + jnp.dot(cq, wrr_ref[h].astype(BF16), preferred_element_type=F32) * sn)
        q_ref[h, :, 0:lora] = q_lat.astype(BF16)
        q_ref[h, :, lora:lora + LANE] = jnp.concatenate([q_rope.astype(BF16), pad], axis=1)


def _mla_proj(zb, cs, sn, qn, kvn, wn, wr, wrr, wuk, layer, *, tm, lora, rope):
    m = zb.shape[0]
    heads, nope = wuk.shape[1], wuk.shape[2]
    dq = lora + LANE

    def full(a):
        nd = a.ndim - 1
        return pl.BlockSpec((None,) + a.shape[1:], lambda i: (layer,) + (0,) * nd)

    return pl.pallas_call(
        functools.partial(_mla_proj_kernel, lora=lora, rope=rope),
        grid=(m // tm,),
        in_specs=[pl.BlockSpec((tm, zb.shape[1]), lambda i: (i, 0)),
                  pl.BlockSpec((tm, rope), lambda i: (i, 0)),
                  pl.BlockSpec((tm, rope), lambda i: (i, 0)),
                  full(qn), full(kvn), full(wn), full(wr), full(wrr), full(wuk)],
        out_specs=[pl.BlockSpec((heads, tm, dq), lambda i: (0, i, 0)),
                   pl.BlockSpec((tm, dq), lambda i: (i, 0)),
                   pl.BlockSpec((tm, lora), lambda i: (i, 0)),
                   pl.BlockSpec((tm, rope), lambda i: (i, 0))],
        out_shape=[jax.ShapeDtypeStruct((heads, m, dq), BF16),
                   jax.ShapeDtypeStruct((m, dq), BF16),
                   jax.ShapeDtypeStruct((m, lora), F32),
                   jax.ShapeDtypeStruct((m, rope), F32)],
        compiler_params=_cparams("parallel"),
        name="mla_proj",
    )(zb, cs, sn, qn, kvn, wn, wr, wrr, wuk)


NEG_BIG = -1e30


def _flash_kernel(q_ref, k_ref, wuv_ref, o_ref, m_ref, l_ref, acc_ref, *, tq, tk, lora, scale):
    qi = pl.program_id(1)
    ki = pl.program_id(2)
    heads = q_ref.shape[0]

    @pl.when(ki == 0)
    def _():
        m_ref[...] = jnp.full(m_ref.shape, NEG_BIG, F32)
        l_ref[...] = jnp.zeros(l_ref.shape, F32)
        acc_ref[...] = jnp.zeros(acc_ref.shape, F32)

    @pl.when(ki * tk <= qi * tq + tq - 1)
    def _():
        q = q_ref[...].reshape(heads * tq, q_ref.shape[2])
        k = k_ref[...]
        s = lax.dot_general(q, k, (((1,), (1,)), ((), ())), preferred_element_type=F32) * scale
        qpos = qi * tq + lax.broadcasted_iota(jnp.int32, s.shape, 0) % tq
        kpos = ki * tk + lax.broadcasted_iota(jnp.int32, s.shape, 1)
        s = jnp.where(kpos <= qpos, s, NEG_BIG)
        m_prev = m_ref[...]
        m_new = jnp.maximum(m_prev, jnp.max(s, axis=-1, keepdims=True))
        alpha = jnp.exp(m_prev - m_new)
        p = jnp.exp(s - m_new)
        l_ref[...] = alpha * l_ref[...] + jnp.sum(p, axis=-1, keepdims=True)
        acc_ref[...] = alpha * acc_ref[...] + jnp.dot(p.astype(BF16), k[:, 0:lora],
                                                      preferred_element_type=F32)
        m_ref[...] = m_new

    @pl.when(ki == pl.num_programs(2) - 1)
    def _():
        o = (acc_ref[...] / l_ref[...]).astype(BF16)
        vdim = wuv_ref.shape[2]
        for h in range(heads):
            o_ref[:, h * vdim:(h + 1) * vdim] = jnp.dot(
                o[h * tq:(h + 1) * tq], wuv_ref[h].astype(BF16),
                preferred_element_type=F32).astype(o_ref.dtype)


def _flash_prompt(q, kall, wuv, layer, *, bsz, t, tq, tk, lora, scale):
    heads, _, dq = q.shape
    vdim = wuv.shape[3]
    nq, nk = t // tq, t // tk

    def k_map(b, i, j):
        return (b * nk + jnp.minimum(j, (i * tq + tq - 1) // tk), 0)

    return pl.pallas_call(
        functools.partial(_flash_kernel, tq=tq, tk=tk, lora=lora, scale=scale),
        grid=(bsz, nq, nk),
        in_specs=[pl.BlockSpec((heads, tq, dq), lambda b, i, j: (0, b * nq + i, 0)),
                  pl.BlockSpec((tk, dq), k_map),
                  pl.BlockSpec((None,) + wuv.shape[1:], lambda b, i, j: (layer, 0, 0, 0))],
        out_specs=pl.BlockSpec((tq, heads * vdim), lambda b, i, j: (b * nq + i, 0)),
        out_shape=jax.ShapeDtypeStruct((bsz * t, heads * vdim), BF16),
        scratch_shapes=[pltpu.VMEM((heads * tq, 1), F32), pltpu.VMEM((heads * tq, 1), F32),
                        pltpu.VMEM((heads * tq, lora), F32)],
        compiler_params=_cparams("parallel", "parallel", "arbitrary"),
        name="flash_prompt",
    )(q, kall, wuv)


def _paged_kernel(pt_ref, q_ref, *refs, npg, lora, rope, n_new, scale):
    ckv_refs = refs[:npg]
    krt_refs = refs[npg:2 * npg]
    knew_ref, o_ref, cbuf_ref, rbuf_ref, m_ref, l_ref, acc_ref = refs[2 * npg:]
    j = pl.program_id(1)
    q = q_ref[0]
    q_lat = q[:, 0:lora]
    q_rope = q[:, lora:lora + rope]

    @pl.when(j == 0)
    def _():
        m_ref[...] = jnp.full(m_ref.shape, NEG_BIG, F32)
        l_ref[...] = jnp.zeros(l_ref.shape, F32)
        acc_ref[...] = jnp.zeros(acc_ref.shape, F32)

    for mpg in range(npg):
        cbuf_ref[pl.ds(mpg * PAGE_SIZE, PAGE_SIZE), :] = ckv_refs[mpg][...].astype(BF16)
        rbuf_ref[:, pl.ds(mpg * PAGE_SIZE, PAGE_SIZE)] = krt_refs[mpg][...].astype(BF16)

    def update(s, vals):
        m_prev = m_ref[...]
        m_new = jnp.maximum(m_prev, jnp.max(s, axis=-1, keepdims=True))
        alpha = jnp.exp(m_prev - m_new)
        p = jnp.exp(s - m_new)
        l_ref[...] = alpha * l_ref[...] + jnp.sum(p, axis=-1, keepdims=True)
        acc_ref[...] = alpha * acc_ref[...] + jnp.dot(p.astype(BF16), vals,
                                                      preferred_element_type=F32)
        m_ref[...] = m_new

    cb = cbuf_ref[...]
    s = (lax.dot_general(q_lat, cb, (((1,), (1,)), ((), ())), preferred_element_type=F32)
         + jnp.dot(q_rope, rbuf_ref[...], preferred_element_type=F32)) * scale
    update(s, cb)

    @pl.when(j == pl.num_programs(1) - 1)
    def _():
        kn = knew_ref[0]
        sn = lax.dot_general(q, kn, (((1,), (1,)), ((), ())), preferred_element_type=F32) * scale
        t_q = lax.broadcasted_iota(jnp.int32, sn.shape, 0) % n_new
        t_k = lax.broadcasted_iota(jnp.int32, sn.shape, 1)
        sn = jnp.where(t_k <= t_q, sn, NEG_BIG)
        update(sn, kn[:, 0:lora])
        o_ref[0] = acc_ref[...] / l_ref[...]


def _paged_attend(page_table, q, cache_ckv, cache_krt, knew, layer, *, npg, lora, rope, n_new, scale):
    bsz, rows, dq = q.shape
    n_pages = page_table.shape[1]
    assert n_pages % npg == 0

    def page_spec(shape, mpg):
        return pl.BlockSpec((None, None) + shape,
                            lambda b, j, pt: (layer, pt[b, j * npg + mpg], 0, 0))

    in_specs = ([pl.BlockSpec((1, rows, dq), lambda b, j, pt: (b, 0, 0))]
                + [page_spec((PAGE_SIZE, lora), mpg) for mpg in range(npg)]
                + [page_spec((rope, PAGE_SIZE), mpg) for mpg in range(npg)]
                + [pl.BlockSpec((1,) + knew.shape[1:], lambda b, j, pt: (b, 0, 0))])
    grid_spec = pltpu.PrefetchScalarGridSpec(
        num_scalar_prefetch=1,
        grid=(bsz, n_pages // npg),
        in_specs=in_specs,
        out_specs=pl.BlockSpec((1, rows, lora), lambda b, j, pt: (b, 0, 0)),
        scratch_shapes=[pltpu.VMEM((npg * PAGE_SIZE, lora), BF16),
                        pltpu.VMEM((rope, npg * PAGE_SIZE), BF16),
                        pltpu.VMEM((rows, 1), F32), pltpu.VMEM((rows, 1), F32),
                        pltpu.VMEM((rows, lora), F32)],
    )
    return pl.pallas_call(
        functools.partial(_paged_kernel, npg=npg, lora=lora, rope=rope, n_new=n_new, scale=scale),
        grid_spec=grid_spec,
        out_shape=jax.ShapeDtypeStruct((bsz, rows, lora), F32),
        compiler_params=_cparams("parallel", "arbitrary"),
        name="paged_attend",
    )(page_table, q, *([cache_ckv] * npg), *([cache_krt] * npg), knew)


def _uv_kernel(o_ref, wuv_ref, out_ref, *, n_new):
    heads, _, vdim = wuv_ref.shape
    nb = o_ref.shape[0]
    for h in range(heads):
        x = o_ref[:, h * n_new:(h + 1) * n_new, :].reshape(nb * n_new, o_ref.shape[2])
        out_ref[:, h * vdim:(h + 1) * vdim] = _bdot(x, wuv_ref[h]).astype(out_ref.dtype)


def _uv_sample(o_lat, wuv, layer, *, nb, n_new):
    bsz, rows, lora = o_lat.shape
    heads, vdim = wuv.shape[1], wuv.shape[3]
    return pl.pallas_call(
        functools.partial(_uv_kernel, n_new=n_new),
        grid=(bsz // nb,),
        in_specs=[pl.BlockSpec((nb, rows, lora), lambda i: (i, 0, 0)),
                  pl.BlockSpec((None,) + wuv.shape[1:], lambda i: (layer, 0, 0, 0))],
        out_specs=pl.BlockSpec((nb * n_new, heads * vdim), lambda i: (i, 0)),
        out_shape=jax.ShapeDtypeStruct((bsz * n_new, heads * vdim), BF16),
        compiler_params=_cparams("parallel"),
        name="uv_sample",
    )(o_lat, wuv)


def _gmlp_kernel(zc_ref, g_ref, b_ref, ws_ref, mask_ref, bias_ref, c_ref, v_ref):
    zc = zc_ref[...]
    zc = 0.5 * zc * (1.0 + lax.erf(zc * np.float32(np.sqrt(0.5))))
    dc = zc.shape[1] // 2
    u = zc[:, 0:dc]
    v = zc[:, dc:]
    vc = v - jnp.mean(v, axis=-1, keepdims=True)
    vn = vc * lax.rsqrt(jnp.mean(vc * vc, axis=-1, keepdims=True) + LN_EPS) * g_ref[...] + b_ref[...]
    v_ref[...] = vn
    mask = mask_ref[...]
    vb = vn.astype(BF16)
    groups = ws_ref.shape[0]
    gd = dc // groups
    for g in range(groups):
        cols = slice(g * gd, (g + 1) * gd)
        wm = jnp.where(mask > 0, ws_ref[g], 0.0).astype(BF16)
        for c in range(zc.shape[0] // CHUNK):
            rows = slice(c * CHUNK, (c + 1) * CHUNK)
            s = jnp.dot(wm, vb[rows, cols], preferred_element_type=F32) + bias_ref[:, cols]
            c_ref[rows, cols] = (u[rows, cols] * s).astype(c_ref.dtype)


def _gmlp(zc, ln_g, ln_b, ws, mask, bias, layer, *, tm, prompt_tiles):
    m, d2 = zc.shape
    dc = d2 // 2
    groups = ws.shape[2]

    def grp(i):
        return jnp.where(i < prompt_tiles, 0, 1)

    vec = pl.BlockSpec((None, 1, dc), lambda i: (layer, 0, 0))
    return pl.pallas_call(
        _gmlp_kernel,
        grid=(m // tm,),
        in_specs=[pl.BlockSpec((tm, d2), lambda i: (i, 0)), vec, vec,
                  pl.BlockSpec((None, None, groups, CHUNK, CHUNK), lambda i: (layer, grp(i), 0, 0, 0)),
                  pl.BlockSpec((None, CHUNK, CHUNK), lambda i: (grp(i), 0, 0)),
                  pl.BlockSpec((None, None, CHUNK, dc), lambda i: (layer, grp(i), 0, 0))],
        out_specs=[pl.BlockSpec((tm, dc), lambda i: (i, 0)), pl.BlockSpec((tm, dc), lambda i: (i, 0))],
        out_shape=[jax.ShapeDtypeStruct((m, dc), BF16), jax.ShapeDtypeStruct((m, dc), F32)],
        compiler_params=_cparams("parallel"),
        name="gmlp",
    )(zc, ln_g, ln_b, ws, mask, bias)


def _merge_kernel(a_ref, b_ref, c_ref, ga_ref, gb_ref, gc_ref, wa_ref, wb_ref, wc_ref, o_ref):
    m = (ga_ref[...] * jnp.dot(a_ref[...], wa_ref[...].astype(BF16), preferred_element_type=F32)
         + gb_ref[...] * jnp.dot(b_ref[...], wb_ref[...].astype(BF16), preferred_element_type=F32)
         + gc_ref[...] * jnp.dot(c_ref[...], wc_ref[...].astype(BF16), preferred_element_type=F32))
    o_ref[...] = m.astype(o_ref.dtype)


def _merge(a, b, c, sg, wa, wb, wc, layer, *, tm, tn):
    m, kd = a.shape
    n = wa.shape[2]
    nj = n // tn
    pre = pl.BlockSpec((tm, kd), lambda i, j: (i, 0))

    def gate(s):
        return pl.BlockSpec((tm, tn), lambda i, j: (i, s * nj + j))

    w = pl.BlockSpec((None, kd, tn), lambda i, j: (layer, 0, j))
    return pl.pallas_call(
        _merge_kernel,
        grid=(m // tm, nj),
        in_specs=[pre, pre, pre, gate(0), gate(1), gate(2), w, w, w],
        out_specs=pl.BlockSpec((tm, tn), lambda i, j: (i, j)),
        out_shape=jax.ShapeDtypeStruct((m, n), BF16),
        compiler_params=_cparams("parallel", "arbitrary"),
        name="merge",
    )(a, b, c, sg, sg, sg, wa, wb, wc)


def _wo_kernel(m_ref, w_ref, x_ref, g_ref, o_ref):
    kk = pl.program_id(1)

    @pl.when(kk == 0)
    def _():
        o_ref[...] = jnp.zeros(o_ref.shape, F32)

    o_ref[...] += jnp.dot(m_ref[...], w_ref[...].astype(BF16), preferred_element_type=F32)

    @pl.when(kk == pl.num_programs(1) - 1)
    def _():
        o_ref[...] = x_ref[...] + _rms(o_ref[...], g_ref[...])


def _wo_norm(mm, w, x, g, layer, *, tm, tk):
    m, d = x.shape
    return pl.pallas_call(
        _wo_kernel,
        grid=(m // tm, d // tk),
        in_specs=[pl.BlockSpec((tm, tk), lambda i, k: (i, k)),
                  pl.BlockSpec((None, tk, d), lambda i, k: (layer, k, 0)),
                  pl.BlockSpec((tm, d), lambda i, k: (i, 0)),
                  pl.BlockSpec((None, 1, d), lambda i, k: (layer, 0, 0))],
        out_specs=pl.BlockSpec((tm, d), lambda i, k: (i, 0)),
        out_shape=jax.ShapeDtypeStruct((m, d), F32),
        compiler_params=_cparams("parallel", "arbitrary"),
        name="wo_norm",
    )(mm, w, x, g)


def _ffn_kernel(x_ref, gpre_ref, up_ref, down_ref, gpost_ref, o_ref, hn_ref):
    f = pl.program_id(1)

    @pl.when(f == 0)
    def _():
        hn_ref[...] = _rms(x_ref[...], gpre_ref[...]).astype(BF16)
        o_ref[...] = jnp.zeros(o_ref.shape, F32)

    a = jnp.dot(hn_ref[...], up_ref[...].astype(BF16), preferred_element_type=F32)
    a = jnp.square(jnp.maximum(a, 0.0))
    o_ref[...] += jnp.dot(a.astype(BF16), down_ref[...].astype(BF16), preferred_element_type=F32)

    @pl.when(f == pl.num_programs(1) - 1)
    def _():
        o_ref[...] = x_ref[...] + _rms(o_ref[...], gpost_ref[...])


def _ffn(x, gpre, up, down, gpost, layer, *, tm, tf):
    m, d = x.shape
    dff = up.shape[2]
    vec = pl.BlockSpec((None, 1, d), lambda i, f: (layer, 0, 0))
    return pl.pallas_call(
        _ffn_kernel,
        grid=(m // tm, dff // tf),
        in_specs=[pl.BlockSpec((tm, d), lambda i, f: (i, 0)), vec,
                  pl.BlockSpec((None, d, tf), lambda i, f: (layer, 0, f)),
                  pl.BlockSpec((None, tf, d), lambda i, f: (layer, f, 0)), vec],
        out_specs=pl.BlockSpec((tm, d), lambda i, f: (i, 0)),
        out_shape=jax.ShapeDtypeStruct((m, d), F32),
        scratch_shapes=[pltpu.VMEM((tm, d), BF16)],
        compiler_params=_cparams("parallel", "arbitrary"),
        name="ffn",
    )(x, gpre, up, down, gpost)


def _ple_kernel(x_ref, xc_ref, p_ref, wg_ref, wp_ref, o_ref, xb_ref):
    @pl.when(pl.program_id(1) == 0)
    def _():
        xb_ref[...] = x_ref[...].astype(BF16)

    gate = jax.nn.sigmoid(jnp.dot(xb_ref[...], wg_ref[...].astype(BF16), preferred_element_type=F32))
    proj = _bdot(p_ref[...], wp_ref[...])
    o_ref[...] = xc_ref[...] + gate * proj


def _ple(x, p, wg, wp, layer, *, tm, tn):
    m, d = x.shape
    pd = p.shape[1]
    return pl.pallas_call(
        _ple_kernel,
        grid=(m // tm, d // tn),
        in_specs=[pl.BlockSpec((tm, d), lambda i, j: (i, 0)),
                  pl.BlockSpec((tm, tn), lambda i, j: (i, j)),
                  pl.BlockSpec((tm, pd), lambda i, j: (i, 0)),
                  pl.BlockSpec((None, d, tn), lambda i, j: (layer, 0, j)),
                  pl.BlockSpec((None, pd, tn), lambda i, j: (layer, 0, j))],
        out_specs=pl.BlockSpec((tm, tn), lambda i, j: (i, j)),
        out_shape=jax.ShapeDtypeStruct((m, d), F32),
        scratch_shapes=[pltpu.VMEM((tm, d), BF16)],
        compiler_params=_cparams("parallel", "arbitrary"),
        name="ple",
    )(x, x, p, wg, wp)


def _scan_consts():
    lane = np.arange(LANE)
    ones_bd = (lane[:, None] // A_HEAD == lane[None, :] // A_HEAD).astype(np.float32)
    eye = (lane[None, :] % A_HEAD == np.arange(A_HEAD)[:, None]).astype(np.float32)
    sel = np.zeros((16, LANE), np.float32)
    sel[0, :A_HEAD] = 1.0
    sel[1, A_HEAD:] = 1.0
    return jnp.asarray(ones_bd, BF16), eye, jnp.asarray(sel, BF16)


def _rope_tables(pos, rope):
    inv_freq = ROPE_THETA ** (-jnp.arange(0, rope, 2, dtype=F32) / rope)
    ang = pos.astype(F32)[:, None] * inv_freq[None, :]
    cos, sin = jnp.cos(ang), jnp.sin(ang)
    return jnp.concatenate([cos, cos], axis=-1), jnp.concatenate([sin, sin], axis=-1)


def _rot_half_cols(w):
    half = w.shape[-1] // 2
    return jnp.concatenate([-w[..., half:], w[..., :half]], axis=-1)


def kernel(x_prompt, x_sample, state_rwkv, state_rwkv_shift, cache_ckv, cache_kr, page_table,
           p_prompt, p_sample, norm_mix_pre, norm_mix_post, norm_ffn_pre, norm_ffn_post, w_in,
           rwkv_mu, rwkv_w0, rwkv_w2, rwkv_a0, rwkv_a2, rwkv_g2, rwkv_k_k, rwkv_k_a, rwkv_r_k,
           rwkv_lnx_g, rwkv_lnx_b, w_out_a, mla_q_norm, mla_w_uq, mla_kv_norm, mla_w_uk, mla_w_uv,
           w_out_b, gmlp_ln_g, gmlp_ln_b, gmlp_w_s, gmlp_b_s, w_out_c, w_o, ffn_up, ffn_down,
           ple_proj, ple_gate):
    depth = w_in.shape[0]
    bsz, seq, d = x_prompt.shape
    dbsz, dseq, _ = x_sample.shape
    mp, ms = bsz * seq, dbsz * dseq
    mt = mp + ms
    da = rwkv_w0.shape[1]
    heads_a = da // A_HEAD
    n_a_in = rwkv_mu.shape[1]
    lora = mla_q_norm.shape[1]
    kv_lora = mla_kv_norm.shape[1]
    assert lora == kv_lora
    b_heads, nope = mla_w_uk.shape[2], mla_w_uk.shape[3]
    rope = mla_w_uq.shape[3] - nope
    vdim = mla_w_uv.shape[3]
    dc = gmlp_ln_g.shape[1]
    groups = gmlp_w_s.shape[1]
    past = page_table.shape[1] * PAGE_SIZE
    scale = float((nope + rope) ** -0.5)
    n_b_in = lora + kv_lora + rope
    col_b = n_a_in
    col_c = n_a_in + n_b_in
    col_g = col_c + 2 * dc
    wa_ext = -(-n_a_in // 512) * 512
    assert 3 * da + 512 == wa_ext and col_c % 512 == 0 and col_g % 512 == 0

    def v3(a):
        return a.reshape(a.shape[0], 1, -1)

    kr0 = col_b + lora + kv_lora
    zpad = jnp.zeros((depth, d, LANE - rope), F32)
    w_b = jnp.concatenate([w_in[:, :, col_b:kr0], w_in[:, :, kr0:kr0 + rope], zpad,
                           _rot_half_cols(w_in[:, :, kr0:kr0 + rope]), zpad], axis=-1)
    wl = jnp.zeros((depth, wa_ext - 3 * da, 3 * da), F32)
    wl = wl.at[:, 0:W_LORA, 0:da].set(rwkv_w2)
    wl = wl.at[:, W_LORA:W_LORA + A_LORA, da:2 * da].set(rwkv_a2)
    wl = wl.at[:, W_LORA + A_LORA:W_LORA + A_LORA + G_LORA, 2 * da:3 * da].set(rwkv_g2)
    wl = wl.astype(BF16)
    mu_ext = jnp.pad(rwkv_mu, ((0, 0), (0, wa_ext - n_a_in)))
    rwkv_vecs = {"mu": v3(mu_ext), "w0": v3(rwkv_w0), "a0": v3(rwkv_a0), "k_k": v3(rwkv_k_k),
                 "k_a": v3(rwkv_k_a), "r_k": v3(rwkv_r_k)}
    w_qn = mla_w_uq[..., :nope].reshape(depth, lora, b_heads * nope)
    w_qr = jnp.transpose(mla_w_uq[..., nope:], (0, 2, 1, 3))
    w_qrr = _rot_half_cols(w_qr)
    w_ukt = jnp.transpose(mla_w_uk, (0, 2, 3, 1))
    w_uvt = jnp.transpose(mla_w_uv, (0, 2, 1, 3))
    seqs_per_chunk = CHUNK // dseq
    tri = np.tril(np.ones((CHUNK, CHUNK), np.float32))
    blk = np.kron(np.eye(seqs_per_chunk, dtype=np.float32), np.tril(np.ones((dseq, dseq), np.float32)))
    sp_mask = jnp.asarray(np.stack([tri, blk]))
    ws_s = jnp.tile(gmlp_w_s[:, :, :dseq, :dseq], (1, 1, seqs_per_chunk, seqs_per_chunk))
    ws_all = jnp.stack([gmlp_w_s, ws_s], axis=1)
    gd = dc // groups
    bias_p = jnp.repeat(jnp.swapaxes(gmlp_b_s, 1, 2), gd, axis=2)
    bias_s = jnp.tile(bias_p[:, :dseq], (1, seqs_per_chunk, 1))
    bias_all = jnp.stack([bias_p, bias_s], axis=1)

    cs_p, sn_p = _rope_tables(jnp.arange(seq), rope)
    cs_s, sn_s = _rope_tables(past + jnp.arange(dseq), rope)
    cs = jnp.concatenate([jnp.tile(cs_p, (bsz, 1)), jnp.tile(cs_s, (dbsz, 1))], axis=0)
    sn = jnp.concatenate([jnp.tile(sn_p, (bsz, 1)), jnp.tile(sn_s, (dbsz, 1))], axis=0)
    ones_bd, eye, sel = _scan_consts()
    cache_krt = jnp.swapaxes(cache_kr, 2, 3)

    x = jnp.concatenate([x_prompt.reshape(mp, d), x_sample.reshape(ms, d)], axis=0)
    ple_all = jnp.concatenate([p_prompt.reshape(depth, mp, -1), p_sample.reshape(depth, ms, -1)], axis=1)

    tm_big = _pick_tile(mt, 1024)
    tm_mid = _pick_tile(mt, 512)
    tm_prep_p = _pick_tile(seq, 256, 8)
    tm_prep_s = _pick_tile(ms, 256, 8)
    tm_g = _pick_tile(int(np.gcd(mp, ms)), 512)
    tq = _pick_tile(seq, 128)
    tk = _pick_tile(seq, 512)
    nb_s = 2 if dbsz % 2 == 0 else 1
    npg = max(n for n in (16, 8, 4, 2, 1) if page_table.shape[1] % n == 0)
    nb_uv = _pick_tile(dbsz, 16, 1)

    outs = {k: [] for k in ("st_p", "sh_p", "ckv_p", "kr_p", "st_s", "sh_s", "ckv_s", "kr_s", "vc_s")}
    for i in range(depth):
        g_pre = v3(norm_mix_pre)
        za = _norm_mm(x, g_pre, w_in, i, col0=0, ncols=wa_ext, tn=512, tm=tm_big)
        zb = _norm_mm(x, g_pre, w_b, i, col0=0, ncols=w_b.shape[2], tn=w_b.shape[2] // 2, tm=tm_big)
        zc = _norm_mm(x, g_pre, w_in, i, col0=col_c, ncols=2 * dc, tn=512, tm=tm_big)
        sg = _norm_mm(x, g_pre, w_in, i, col0=col_g, ncols=3 * d, tn=512, tm=tm_big, act="sigmoid")

        shift0 = jnp.pad(state_rwkv_shift[i], ((0, 0), (0, wa_ext - n_a_in)))
        shift_rows = jnp.repeat(shift0, dseq, axis=0)
        prep_p = _rwkv_prep(za, za, rwkv_vecs, wl, ones_bd, i, row0=0, nrows=mp, tm=tm_prep_p,
                            seq_len=seq, aux_is_carry=True)
        prep_s = _rwkv_prep(za, shift_rows, rwkv_vecs, wl, ones_bd, i, row0=mp, nrows=ms,
                            tm=tm_prep_s, seq_len=dseq, aux_is_carry=False)
        def scan_consts(nb):
            return [ones_bd, jnp.asarray(np.tile(eye, (nb * heads_a // 2, 1))), sel]

        y_p, st_p = _rwkv_scan(None, [a.reshape(bsz, seq, da) for a in prep_p[:6]],
                               scan_consts(bsz), i, nb=bsz, tc=8)
        y_s, st_s = _rwkv_scan(state_rwkv, [a.reshape(dbsz, dseq, da) for a in prep_s[:6]],
                               scan_consts(nb_s), i, nb=nb_s, tc=dseq)
        a_p = _rwkv_post(y_p.reshape(mp, da), prep_p[7], prep_p[6], v3(rwkv_lnx_g), v3(rwkv_lnx_b),
                         ones_bd, i, tm=_pick_tile(mp, 512))
        a_s = _rwkv_post(y_s.reshape(ms, da), prep_s[7], prep_s[6], v3(rwkv_lnx_g), v3(rwkv_lnx_b),
                         ones_bd, i, tm=_pick_tile(ms, 512))
        a_pre = jnp.concatenate([a_p, a_s], axis=0)
        sh_p = za[seq - 1:mp:seq, :n_a_in]
        sh_s = za[mp + dseq - 1::dseq, :n_a_in]

        q_all, k_all, ckv, kr = _mla_proj(zb, cs, sn, v3(mla_q_norm), v3(mla_kv_norm), w_qn, w_qr,
                                          w_qrr, w_ukt, i, tm=tm_mid, lora=lora, rope=rope)
        b_p = _flash_prompt(q_all, k_all, w_uvt, i, bsz=bsz, t=seq, tq=tq, tk=tk, lora=lora,
                            scale=scale)
        q_s = q_all[:, mp:].reshape(b_heads, dbsz, dseq, -1)
        q_s = jnp.transpose(q_s, (1, 0, 2, 3)).reshape(dbsz, b_heads * dseq, -1)
        knew = jnp.pad(k_all[mp:].reshape(dbsz, dseq, -1), ((0, 0), (0, 16 - dseq), (0, 0)))
        o_s = _paged_attend(page_table, q_s, cache_ckv, cache_krt, knew, i, npg=npg, lora=lora,
                            rope=rope, n_new=dseq, scale=scale)
        b_s = _uv_sample(o_s, w_uvt, i, nb=nb_uv, n_new=dseq)
        b_pre = jnp.concatenate([b_p, b_s], axis=0)

        c_pre, v_c = _gmlp(zc, v3(gmlp_ln_g), v3(gmlp_ln_b), ws_all, sp_mask, bias_all, i,
                           tm=tm_g, prompt_tiles=mp // tm_g)

        mm = _merge(a_pre, b_pre, c_pre, sg, w_out_a, w_out_b, w_out_c, i, tm=tm_big, tn=512)
        x = _wo_norm(mm, w_o, x, v3(norm_mix_post), i, tm=tm_big, tk=512)
        x = _ffn(x, v3(norm_ffn_pre), ffn_up, ffn_down, v3(norm_ffn_post), i, tm=tm_big, tf=256)
        x = _ple(x, ple_all[i], ple_gate, ple_proj, i, tm=tm_big, tn=512)

        outs["st_p"].append(st_p)
        outs["sh_p"].append(sh_p)
        outs["ckv_p"].append(ckv[:mp].reshape(bsz, seq, kv_lora))
        outs["kr_p"].append(kr[:mp].reshape(bsz, seq, rope))
        outs["st_s"].append(st_s)
        outs["sh_s"].append(sh_s)
        outs["ckv_s"].append(ckv[mp:].reshape(dbsz, dseq, kv_lora))
        outs["kr_s"].append(kr[mp:].reshape(dbsz, dseq, rope))
        outs["vc_s"].append(v_c[mp:].reshape(dbsz, dseq, dc))

    return (x[:mp].reshape(bsz, seq, d), x[mp:].reshape(dbsz, dseq, d),
            jnp.stack(outs["st_p"]), jnp.stack(outs["sh_p"]), jnp.stack(outs["ckv_p"]),
            jnp.stack(outs["kr_p"]), jnp.stack(outs["st_s"]), jnp.stack(outs["sh_s"]),
            jnp.stack(outs["ckv_s"]), jnp.stack(outs["kr_s"]), jnp.stack(outs["vc_s"]))
```

```python
import functools

import jax
import jax.numpy as jnp
import numpy as np
from jax import lax
from jax.experimental import pallas as pl
from jax.experimental.pallas import tpu as pltpu

F32 = jnp.float32
BF16 = jnp.bfloat16

NORM_EPS = 1e-6
LNX_EPS = 64e-5
LN_EPS = 1e-5
ROPE_THETA = 10000.0
PAGE_SIZE = 128
CHUNK = 128
A_HEAD = 64
W_LORA = 96
A_LORA = 96
G_LORA = 256
LANE = 128
VMEM_LIMIT = 56 * 1024 * 1024


def _cparams(*sem):
    return pltpu.CompilerParams(dimension_semantics=sem, vmem_limit_bytes=VMEM_LIMIT)


def _pick_tile(m, target, quantum=LANE):
    t = min(target, m)
    t -= t % quantum
    while m % t:
        t -= quantum
    return t


def _rms(x, g):
    return x * lax.rsqrt(jnp.mean(x * x, axis=-1, keepdims=True) + NORM_EPS) * g


def _bdot(a, b):
    return jnp.dot(a.astype(BF16), b.astype(BF16), preferred_element_type=F32)


def _seg_sum(x, ones_bd):
    outs = []
    for c in range(x.shape[1] // LANE):
        xb = x[:, c * LANE:(c + 1) * LANE]
        hi = xb.astype(BF16)
        r1 = xb - hi.astype(F32)
        mid = r1.astype(BF16)
        lo = (r1 - mid.astype(F32)).astype(BF16)
        s = (jnp.dot(hi, ones_bd, preferred_element_type=F32)
             + jnp.dot(mid, ones_bd, preferred_element_type=F32)
             + jnp.dot(lo, ones_bd, preferred_element_type=F32))
        outs.append(s)
    return jnp.concatenate(outs, axis=1)


def _norm_mm_kernel(x_ref, g_ref, w_ref, o_ref, xn_ref, *, act):
    @pl.when(pl.program_id(1) == 0)
    def _():
        xn_ref[...] = _rms(x_ref[...], g_ref[...]).astype(BF16)

    y = jnp.dot(xn_ref[...], w_ref[...].astype(BF16), preferred_element_type=F32)
    if act == "sigmoid":
        y = jax.nn.sigmoid(y)
    o_ref[...] = y.astype(o_ref.dtype)


def _norm_mm(x, g, w, layer, *, col0, ncols, tn, tm, act=None):
    m, k = x.shape
    j0 = col0 // tn
    assert col0 % tn == 0 and ncols % tn == 0
    w_spec = pl.BlockSpec((None, k, tn), lambda i, j: (layer, 0, j0 + j))
    return pl.pallas_call(
        functools.partial(_norm_mm_kernel, act=act),
        grid=(m // tm, ncols // tn),
        in_specs=[pl.BlockSpec((tm, k), lambda i, j: (i, 0)),
                  pl.BlockSpec((None, 1, k), lambda i, j: (layer, 0, 0)),
                  w_spec],
        out_specs=pl.BlockSpec((tm, tn), lambda i, j: (i, j)),
        out_shape=jax.ShapeDtypeStruct((m, ncols), F32),
        scratch_shapes=[pltpu.VMEM((tm, k), BF16)],
        compiler_params=_cparams("parallel", "arbitrary"),
        name="norm_mm",
    )(x, g, w)


def _rwkv_prep_kernel(za_ref, aux_ref, mu_ref, wl_ref, w0_ref, a0_ref, kkw_ref, ka_ref, rk_ref,
                      ones_ref, r_ref, w_ref, k_ref, v_ref, nkk_ref, b_ref, gate_ref, bonus_ref,
                      *, seq_tiles, rows_per_seq):
    za = za_ref[...]
    tm = za.shape[0]
    row = lax.broadcasted_iota(jnp.int32, za.shape, 0)
    rolled = pltpu.roll(za, 1, 0)
    if seq_tiles is not None:
        first = jnp.where(pl.program_id(0) % seq_tiles == 0, 0.0, aux_ref[7:8, :])
        prev = jnp.where(row == 0, first, rolled)
    else:
        prev = jnp.where(row % rows_per_seq == 0, aux_ref[...], rolled)
    zs = za + mu_ref[...] * (prev - za)
    da = w0_ref.shape[1]
    r = zs[:, 0:da]
    k = zs[:, da:2 * da]
    v = zs[:, 2 * da:3 * da]
    lr = zs[:, 3 * da:]
    col = lax.broadcasted_iota(jnp.int32, lr.shape, 1)
    lact = jnp.where(col < W_LORA, jnp.tanh(lr),
                     jnp.where(col < W_LORA + A_LORA, lr, jax.nn.sigmoid(lr)))
    lo = jnp.dot(lact.astype(BF16), wl_ref[...], preferred_element_type=F32)
    w_log = -jax.nn.softplus(-(w0_ref[...] + lo[:, 0:da])) - 0.5
    decay = jnp.exp(-jnp.exp(w_log))
    a = jax.nn.sigmoid(a0_ref[...] + lo[:, da:2 * da])
    gate = lo[:, 2 * da:3 * da]
    ones_bd = ones_ref[...]
    kk = k * kkw_ref[...]
    kk = kk * lax.rsqrt(jnp.maximum(_seg_sum(kk * kk, ones_bd), 1e-24))
    k2 = k * (1.0 + (a - 1.0) * ka_ref[...])
    r_ref[...] = r
    w_ref[...] = decay
    k_ref[...] = k2
    v_ref[...] = v
    nkk_ref[...] = -kk
    b_ref[...] = kk * a
    gate_ref[...] = gate
    bonus_ref[...] = _seg_sum(r * k2 * rk_ref[...], ones_bd) * v


def _rwkv_prep(za_all, aux, vecs, wl, ones_bd, layer, *, row0, nrows, tm, seq_len, aux_is_carry):
    wa = za_all.shape[1]
    da = vecs["w0"].shape[-1]
    i0 = row0 // tm
    if aux_is_carry:
        seq_tiles = seq_len // tm
        c0 = row0 // 8
        aux_spec = pl.BlockSpec((8, wa), lambda i: (jnp.maximum(c0 + i * (tm // 8) - 1, 0), 0))
        rows_per_seq = None
    else:
        seq_tiles = None
        rows_per_seq = seq_len
        aux_spec = pl.BlockSpec((tm, wa), lambda i: (i, 0))

    def vec(n):
        return pl.BlockSpec((None, 1, n), lambda i: (layer, 0, 0))

    out_spec = pl.BlockSpec((tm, da), lambda i: (i, 0))
    out_sds = jax.ShapeDtypeStruct((nrows, da), F32)
    out_specs, out_shape = [out_spec] * 8, [out_sds] * 8
    return pl.pallas_call(
        functools.partial(_rwkv_prep_kernel, seq_tiles=seq_tiles, rows_per_seq=rows_per_seq),
        grid=(nrows // tm,),
        in_specs=[pl.BlockSpec((tm, wa), lambda i: (i0 + i, 0)), aux_spec, vec(wa),
                  pl.BlockSpec((None,) + wl.shape[1:], lambda i: (layer, 0, 0)),
                  vec(da), vec(da), vec(da), vec(da), vec(da),
                  pl.BlockSpec(ones_bd.shape, lambda i: (0, 0))],
        out_specs=out_specs,
        out_shape=out_shape,
        compiler_params=_cparams("parallel"),
        name="rwkv_prep",
    )(za_all, aux, vecs["mu"], wl, vecs["w0"], vecs["a0"], vecs["k_k"], vecs["k_a"], vecs["r_k"],
      ones_bd)


def _rwkv_scan_kernel(*refs, nb, zero_init):
    if zero_init:
        s0_ref = None
        (r_ref, w_ref, k_ref, v_ref, nkk_ref, b_ref, ones_ref, eye_ref, sel_ref,
         y_ref, so_ref, st_ref) = refs
    else:
        (s0_ref, r_ref, w_ref, k_ref, v_ref, nkk_ref, b_ref, ones_ref, eye_ref, sel_ref,
         y_ref, so_ref, st_ref) = refs
    ci = pl.program_id(1)
    tc = r_ref.shape[1]
    npair = r_ref.shape[2] // LANE
    nq = nb * npair

    @pl.when(ci == 0)
    def _():
        for bb in range(nb):
            for p in range(npair):
                rows = pl.ds((bb * npair + p) * A_HEAD, A_HEAD)
                if zero_init:
                    st_ref[rows, :] = jnp.zeros((A_HEAD, LANE), F32)
                else:
                    st_ref[rows, :] = jnp.concatenate(
                        [s0_ref[bb, 2 * p], s0_ref[bb, 2 * p + 1]], axis=1)

    ones_bd = ones_ref[...]
    eye = eye_ref[...]
    sel = sel_ref[...]
    low_half = lax.broadcasted_iota(jnp.int32, (1, LANE), 1) < A_HEAD

    def rows_of(ref, t):
        return jnp.concatenate(
            [jnp.broadcast_to(ref[bb, pl.ds(t, 1), pl.ds(p * LANE, LANE)], (A_HEAD, LANE))
             for bb in range(nb) for p in range(npair)], axis=0)

    for t in range(tc):
        s_old = st_ref[...]
        m1 = (s_old * rows_of(nkk_ref, t)).astype(BF16)
        sa = jnp.dot(m1, ones_bd, preferred_element_type=F32)
        d = (eye * rows_of(v_ref, t)).astype(BF16)
        v_bc = jnp.dot(d, ones_bd, preferred_element_type=F32)
        s_new = s_old * rows_of(w_ref, t) + sa * rows_of(b_ref, t) + v_bc * rows_of(k_ref, t)
        st_ref[...] = s_new
        m2 = (s_new * rows_of(r_ref, t)).astype(BF16)
        y16 = lax.dot_general(sel, m2, (((1,), (1,)), ((), ())), preferred_element_type=F32)
        for j in range(nq // 2):
            r0 = y16[0:1, j * LANE:(j + 1) * LANE]
            r1 = y16[1:2, j * LANE:(j + 1) * LANE]
            even = jnp.where(low_half, r0, pltpu.roll(r1, A_HEAD, 1))
            odd = jnp.where(low_half, pltpu.roll(r0, A_HEAD, 1), r1)
            for q, val in ((2 * j, even), (2 * j + 1, odd)):
                y_ref[q // npair, pl.ds(t, 1), pl.ds((q % npair) * LANE, LANE)] = val

    @pl.when(ci == pl.num_programs(1) - 1)
    def _():
        for bb in range(nb):
            for p in range(npair):
                s = st_ref[pl.ds((bb * npair + p) * A_HEAD, A_HEAD), :]
                so_ref[bb, 2 * p] = s[:, :A_HEAD]
                so_ref[bb, 2 * p + 1] = s[:, A_HEAD:]


def _rwkv_scan(s0, seqs, consts, layer, *, nb, tc):
    bsz, t, da = seqs[0].shape
    seq_spec = pl.BlockSpec((nb, tc, da), lambda g, c: (g, c, 0))
    heads = da // A_HEAD
    zero_init = s0 is None
    st_spec = pl.BlockSpec((nb, heads, A_HEAD, A_HEAD), lambda g, c: (g, 0, 0, 0))
    const_specs = [pl.BlockSpec(c.shape, lambda g, c_: (0, 0)) for c in consts]
    s0_spec = pl.BlockSpec((None, nb, heads, A_HEAD, A_HEAD), lambda g, c: (layer, g, 0, 0, 0))
    in_specs = ([] if zero_init else [s0_spec]) + [seq_spec] * 6 + const_specs
    args = ([] if zero_init else [s0]) + list(seqs) + list(consts)
    return pl.pallas_call(
        functools.partial(_rwkv_scan_kernel, nb=nb, zero_init=zero_init),
        grid=(bsz // nb, t // tc),
        in_specs=in_specs,
        out_specs=[seq_spec, st_spec],
        out_shape=[jax.ShapeDtypeStruct(seqs[0].shape, F32),
                   jax.ShapeDtypeStruct((bsz, heads, A_HEAD, A_HEAD), F32)],
        scratch_shapes=[pltpu.VMEM((nb * da // LANE * A_HEAD, LANE), F32)],
        compiler_params=_cparams("parallel", "arbitrary"),
        name="rwkv_scan",
    )(*args)


def _rwkv_post_kernel(y_ref, bonus_ref, gate_ref, g_ref, b_ref, ones_ref, o_ref):
    y = y_ref[...]
    ones_bd = ones_ref[...]
    yc = y - _seg_sum(y, ones_bd) * (1.0 / A_HEAD)
    var = _seg_sum(yc * yc, ones_bd) * (1.0 / A_HEAD)
    yn = yc * lax.rsqrt(var + LNX_EPS) * g_ref[...] + b_ref[...]
    o_ref[...] = ((yn + bonus_ref[...]) * gate_ref[...]).astype(o_ref.dtype)


def _rwkv_post(y, bonus, gate, lnx_g, lnx_b, ones_bd, layer, *, tm):
    m, da = bonus.shape
    row = pl.BlockSpec((tm, da), lambda i: (i, 0))
    y_spec = row
    vec = pl.BlockSpec((None, 1, da), lambda i: (layer, 0, 0))
    return pl.pallas_call(
        _rwkv_post_kernel,
        grid=(m // tm,),
        in_specs=[y_spec, row, row, vec, vec, pl.BlockSpec(ones_bd.shape, lambda i: (0, 0))],
        out_specs=row,
        out_shape=jax.ShapeDtypeStruct((m, da), BF16),
        compiler_params=_cparams("parallel"),
        name="rwkv_post",
    )(y, bonus, gate, lnx_g, lnx_b, ones_bd)


def _mla_proj_kernel(zb_ref, cs_ref, sn_ref, qn_ref, kvn_ref, wn_ref, wr_ref, wrr_ref, wuk_ref,
                     q_ref, kall_ref, ckv_ref, kr_ref, *, lora, rope):
    zb = zb_ref[...]
    cs = cs_ref[...]
    sn = sn_ref[...]
    cq = _rms(zb[:, 0:lora], qn_ref[...]).astype(BF16)
    ckv = _rms(zb[:, lora:2 * lora], kvn_ref[...])
    kr = zb[:, 2 * lora:2 * lora + rope] * cs + zb[:, 2 * lora + LANE:2 * lora + LANE + rope] * sn
    ckv_ref[...] = ckv
    kr_ref[...] = kr
    pad = jnp.zeros((zb.shape[0], LANE - rope), BF16)
    kall_ref[:, 0:lora] = ckv.astype(BF16)
    kall_ref[:, lora:lora + LANE] = jnp.concatenate([kr.astype(BF16), pad], axis=1)
    heads = wuk_ref.shape[0]
    nope = wuk_ref.shape[1]
    qn = jnp.dot(cq, wn_ref[...].astype(BF16), preferred_element_type=F32)
    for h in range(heads):
        q_lat = _bdot(qn[:, h * nope:(h + 1) * nope], wuk_ref[h])
        q_rope = (jnp.dot(cq, wr_ref[h].astype(BF16), preferred_element_type=F32) * cs
                  + jnp.dot(cq, wrr_ref[h].astype(BF16), preferred_element_type=F32) * sn)
        q_ref[h, :, 0:lora] = q_lat.astype(BF16)
        q_ref[h, :, lora:lora + LANE] = jnp.concatenate([q_rope.astype(BF16), pad], axis=1)


def _mla_proj(zb, cs, sn, qn, kvn, wn, wr, wrr, wuk, layer, *, tm, lora, rope):
    m = zb.shape[0]
    heads, nope = wuk.shape[1], wuk.shape[2]
    dq = lora + LANE

    def full(a):
        nd = a.ndim - 1
        return pl.BlockSpec((None,) + a.shape[1:], lambda i: (layer,) + (0,) * nd)

    return pl.pallas_call(
        functools.partial(_mla_proj_kernel, lora=lora, rope=rope),
        grid=(m // tm,),
        in_specs=[pl.BlockSpec((tm, zb.shape[1]), lambda i: (i, 0)),
                  pl.BlockSpec((tm, rope), lambda i: (i, 0)),
                  pl.BlockSpec((tm, rope), lambda i: (i, 0)),
                  full(qn), full(kvn), full(wn), full(wr), full(wrr), full(wuk)],
        out_specs=[pl.BlockSpec((heads, tm, dq), lambda i: (0, i, 0)),
                   pl.BlockSpec((tm, dq), lambda i: (i, 0)),
                   pl.BlockSpec((tm, lora), lambda i: (i, 0)),
                   pl.BlockSpec((tm, rope), lambda i: (i, 0))],
        out_shape=[jax.ShapeDtypeStruct((heads, m, dq), BF16),
                   jax.ShapeDtypeStruct((m, dq), BF16),
                   jax.ShapeDtypeStruct((m, lora), F32),
                   jax.ShapeDtypeStruct((m, rope), F32)],
        compiler_params=_cparams("parallel"),
        name="mla_proj",
    )(zb, cs, sn, qn, kvn, wn, wr, wrr, wuk)


NEG_BIG = -1e30


def _flash_kernel(q_ref, k_ref, wuv_ref, o_ref, m_ref, l_ref, acc_ref, *, tq, tk, lora, scale, nsplit):
    qi = pl.program_id(1)
    heads = q_ref.shape[0]
    rows = heads * tq
    m_ref[...] = jnp.full(m_ref.shape, NEG_BIG, F32)
    l_ref[...] = jnp.zeros(l_ref.shape, F32)
    acc_ref[...] = jnp.zeros(acc_ref.shape, F32)

    def tile(ki, masked):
        k = k_ref[pl.ds(pl.multiple_of(ki * tk, tk), tk), :]
        for part in range(nsplit):
            hs = heads // nsplit
            rs = pl.ds(part * hs * tq, hs * tq)
            q = q_ref[part * hs:(part + 1) * hs].reshape(hs * tq, q_ref.shape[2])
            s = lax.dot_general(q, k, (((1,), (1,)), ((), ())), preferred_element_type=F32) * scale
            if masked:
                qpos = qi * tq + lax.broadcasted_iota(jnp.int32, s.shape, 0) % tq
                kpos = ki * tk + lax.broadcasted_iota(jnp.int32, s.shape, 1)
                s = jnp.where(kpos <= qpos, s, NEG_BIG)
            m_prev = m_ref[rs, :]
            m_new = jnp.maximum(m_prev, jnp.max(s, axis=-1, keepdims=True))
            alpha = jnp.exp(m_prev - m_new)
            p = jnp.exp(s - m_new)
            l_ref[rs, :] = alpha * l_ref[rs, :] + jnp.sum(p, axis=-1, keepdims=True)
            acc_ref[rs, :] = alpha * acc_ref[rs, :] + jnp.dot(p.astype(BF16), k[:, 0:lora],
                                                              preferred_element_type=F32)
            m_ref[rs, :] = m_new

    n_full = (qi * tq) // tk

    def body(ki, carry):
        tile(ki, False)
        return carry

    lax.fori_loop(0, n_full, body, 0)
    tile(n_full, True)

    o = (acc_ref[...] / l_ref[...]).astype(BF16)
    vdim = wuv_ref.shape[2]
    for h in range(heads):
        o_ref[:, h * vdim:(h + 1) * vdim] = jnp.dot(
            o[h * tq:(h + 1) * tq], wuv_ref[h].astype(BF16),
            preferred_element_type=F32).astype(o_ref.dtype)


def _flash_prompt(q, kall, wuv, layer, *, bsz, t, tq, tk, lora, scale):
    heads, _, dq = q.shape
    vdim = wuv.shape[3]
    nq = t // tq
    assert tk % tq == 0 and t % tk == 0
    return pl.pallas_call(
        functools.partial(_flash_kernel, tq=tq, tk=tk, lora=lora, scale=scale,
                          nsplit=2 if heads % 2 == 0 else 1),
        grid=(bsz, nq),
        in_specs=[pl.BlockSpec((heads, tq, dq), lambda b, i: (0, b * nq + i, 0)),
                  pl.BlockSpec((t, dq), lambda b, i: (b, 0)),
                  pl.BlockSpec((None,) + wuv.shape[1:], lambda b, i: (layer, 0, 0, 0))],
        out_specs=pl.BlockSpec((tq, heads * vdim), lambda b, i: (b * nq + i, 0)),
        out_shape=jax.ShapeDtypeStruct((bsz * t, heads * vdim), BF16),
        scratch_shapes=[pltpu.VMEM((heads * tq, 1), F32), pltpu.VMEM((heads * tq, 1), F32),
                        pltpu.VMEM((heads * tq, lora), F32)],
        compiler_params=_cparams("parallel", "arbitrary"),
        name="flash_prompt",
    )(q, kall, wuv)


def _paged_kernel(pt_ref, q_ref, *refs, npg, lora, rope, n_new, scale):
    ckv_refs = refs[:npg]
    krt_refs = refs[npg:2 * npg]
    knew_ref, o_ref, cbuf_ref, rbuf_ref, m_ref, l_ref, acc_ref = refs[2 * npg:]
    j = pl.program_id(1)
    q = q_ref[0]
    q_lat = q[:, 0:lora]
    q_rope = q[:, lora:lora + rope]

    @pl.when(j == 0)
    def _():
        m_ref[...] = jnp.full(m_ref.shape, NEG_BIG, F32)
        l_ref[...] = jnp.zeros(l_ref.shape, F32)
        acc_ref[...] = jnp.zeros(acc_ref.shape, F32)

    for mpg in range(npg):
        cbuf_ref[pl.ds(mpg * PAGE_SIZE, PAGE_SIZE), :] = ckv_refs[mpg][...].astype(BF16)
        rbuf_ref[:, pl.ds(mpg * PAGE_SIZE, PAGE_SIZE)] = krt_refs[mpg][...].astype(BF16)

    def update(s, vals):
        m_prev = m_ref[...]
        m_new = jnp.maximum(m_prev, jnp.max(s, axis=-1, keepdims=True))
        alpha = jnp.exp(m_prev - m_new)
        p = jnp.exp(s - m_new)
        l_ref[...] = alpha * l_ref[...] + jnp.sum(p, axis=-1, keepdims=True)
        acc_ref[...] = alpha * acc_ref[...] + jnp.dot(p.astype(BF16), vals,
                                                      preferred_element_type=F32)
        m_ref[...] = m_new

    cb = cbuf_ref[...]
    s = (lax.dot_general(q_lat, cb, (((1,), (1,)), ((), ())), preferred_element_type=F32)
         + jnp.dot(q_rope, rbuf_ref[...], preferred_element_type=F32)) * scale
    update(s, cb)

    @pl.when(j == pl.num_programs(1) - 1)
    def _():
        kn = knew_ref[0]
        sn = lax.dot_general(q, kn, (((1,), (1,)), ((), ())), preferred_element_type=F32) * scale
        t_q = lax.broadcasted_iota(jnp.int32, sn.shape, 0) % n_new
        t_k = lax.broadcasted_iota(jnp.int32, sn.shape, 1)
        sn = jnp.where(t_k <= t_q, sn, NEG_BIG)
        update(sn, kn[:, 0:lora])
        o_ref[0] = acc_ref[...] / l_ref[...]


def _paged_attend(page_table, q, cache_ckv, cache_krt, knew, layer, *, npg, lora, rope, n_new, scale):
    bsz, rows, dq = q.shape
    n_pages = page_table.shape[1]
    assert n_pages % npg == 0

    def page_spec(shape, mpg):
        return pl.BlockSpec((None, None) + shape,
                            lambda b, j, pt: (layer, pt[b, j * npg + mpg], 0, 0))

    in_specs = ([pl.BlockSpec((1, rows, dq), lambda b, j, pt: (b, 0, 0))]
                + [page_spec((PAGE_SIZE, lora), mpg) for mpg in range(npg)]
                + [page_spec((rope, PAGE_SIZE), mpg) for mpg in range(npg)]
                + [pl.BlockSpec((1,) + knew.shape[1:], lambda b, j, pt: (b, 0, 0))])
    grid_spec = pltpu.PrefetchScalarGridSpec(
        num_scalar_prefetch=1,
        grid=(bsz, n_pages // npg),
        in_specs=in_specs,
        out_specs=pl.BlockSpec((1, rows, lora), lambda b, j, pt: (b, 0, 0)),
        scratch_shapes=[pltpu.VMEM((npg * PAGE_SIZE, lora), BF16),
                        pltpu.VMEM((rope, npg * PAGE_SIZE), BF16),
                        pltpu.VMEM((rows, 1), F32), pltpu.VMEM((rows, 1), F32),
                        pltpu.VMEM((rows, lora), F32)],
    )
    return pl.pallas_call(
        functools.partial(_paged_kernel, npg=npg, lora=lora, rope=rope, n_new=n_new, scale=scale),
        grid_spec=grid_spec,
        out_shape=jax.ShapeDtypeStruct((bsz, rows, lora), F32),
        compiler_params=_cparams("parallel", "arbitrary"),
        name="paged_attend",
    )(page_table, q, *([cache_ckv] * npg), *([cache_krt] * npg), knew)


def _uv_kernel(o_ref, wuv_ref, out_ref, *, n_new):
    heads, _, vdim = wuv_ref.shape
    nb = o_ref.shape[0]
    for h in range(heads):
        x = o_ref[:, h * n_new:(h + 1) * n_new, :].reshape(nb * n_new, o_ref.shape[2])
        out_ref[:, h * vdim:(h + 1) * vdim] = _bdot(x, wuv_ref[h]).astype(out_ref.dtype)


def _uv_sample(o_lat, wuv, layer, *, nb, n_new):
    bsz, rows, lora = o_lat.shape
    heads, vdim = wuv.shape[1], wuv.shape[3]
    return pl.pallas_call(
        functools.partial(_uv_kernel, n_new=n_new),
        grid=(bsz // nb,),
        in_specs=[pl.BlockSpec((nb, rows, lora), lambda i: (i, 0, 0)),
                  pl.BlockSpec((None,) + wuv.shape[1:], lambda i: (layer, 0, 0, 0))],
        out_specs=pl.BlockSpec((nb * n_new, heads * vdim), lambda i: (i, 0)),
        out_shape=jax.ShapeDtypeStruct((bsz * n_new, heads * vdim), BF16),
        compiler_params=_cparams("parallel"),
        name="uv_sample",
    )(o_lat, wuv)


def _gmlp_kernel(zc_ref, g_ref, b_ref, ws_ref, mask_ref, bias_ref, c_ref, v_ref):
    zc = zc_ref[...]
    zc = 0.5 * zc * (1.0 + lax.erf(zc * np.float32(np.sqrt(0.5))))
    dc = zc.shape[1] // 2
    u = zc[:, 0:dc]
    v = zc[:, dc:]
    vc = v - jnp.mean(v, axis=-1, keepdims=True)
    vn = vc * lax.rsqrt(jnp.mean(vc * vc, axis=-1, keepdims=True) + LN_EPS) * g_ref[...] + b_ref[...]
    v_ref[...] = vn
    mask = mask_ref[...]
    vb = vn.astype(BF16)
    groups = ws_ref.shape[0]
    gd = dc // groups
    for g in range(groups):
        cols = slice(g * gd, (g + 1) * gd)
        wm = jnp.where(mask > 0, ws_ref[g], 0.0).astype(BF16)
        for c in range(zc.shape[0] // CHUNK):
            rows = slice(c * CHUNK, (c + 1) * CHUNK)
            s = jnp.dot(wm, vb[rows, cols], preferred_element_type=F32) + bias_ref[:, cols]
            c_ref[rows, cols] = (u[rows, cols] * s).astype(c_ref.dtype)


def _gmlp(zc, ln_g, ln_b, ws, mask, bias, layer, *, tm, prompt_tiles):
    m, d2 = zc.shape
    dc = d2 // 2
    groups = ws.shape[2]

    def grp(i):
        return jnp.where(i < prompt_tiles, 0, 1)

    vec = pl.BlockSpec((None, 1, dc), lambda i: (layer, 0, 0))
    return pl.pallas_call(
        _gmlp_kernel,
        grid=(m // tm,),
        in_specs=[pl.BlockSpec((tm, d2), lambda i: (i, 0)), vec, vec,
                  pl.BlockSpec((None, None, groups, CHUNK, CHUNK), lambda i: (layer, grp(i), 0, 0, 0)),
                  pl.BlockSpec((None, CHUNK, CHUNK), lambda i: (grp(i), 0, 0)),
                  pl.BlockSpec((None, None, CHUNK, dc), lambda i: (layer, grp(i), 0, 0))],
        out_specs=[pl.BlockSpec((tm, dc), lambda i: (i, 0)), pl.BlockSpec((tm, dc), lambda i: (i, 0))],
        out_shape=[jax.ShapeDtypeStruct((m, dc), BF16), jax.ShapeDtypeStruct((m, dc), F32)],
        compiler_params=_cparams("parallel"),
        name="gmlp",
    )(zc, ln_g, ln_b, ws, mask, bias)


def _merge_kernel(a_ref, b_ref, c_ref, ga_ref, gb_ref, gc_ref, wa_ref, wb_ref, wc_ref, o_ref):
    m = (ga_ref[...] * jnp.dot(a_ref[...], wa_ref[...].astype(BF16), preferred_element_type=F32)
         + gb_ref[...] * jnp.dot(b_ref[...], wb_ref[...].astype(BF16), preferred_element_type=F32)
         + gc_ref[...] * jnp.dot(c_ref[...], wc_ref[...].astype(BF16), preferred_element_type=F32))
    o_ref[...] = m.astype(o_ref.dtype)


def _merge(a, b, c, sg, wa, wb, wc, layer, *, tm, tn):
    m, kd = a.shape
    n = wa.shape[2]
    nj = n // tn
    pre = pl.BlockSpec((tm, kd), lambda i, j: (i, 0))

    def gate(s):
        return pl.BlockSpec((tm, tn), lambda i, j: (i, s * nj + j))

    w = pl.BlockSpec((None, kd, tn), lambda i, j: (layer, 0, j))
    return pl.pallas_call(
        _merge_kernel,
        grid=(m // tm, nj),
        in_specs=[pre, pre, pre, gate(0), gate(1), gate(2), w, w, w],
        out_specs=pl.BlockSpec((tm, tn), lambda i, j: (i, j)),
        out_shape=jax.ShapeDtypeStruct((m, n), BF16),
        compiler_params=_cparams("parallel", "arbitrary"),
        name="merge",
    )(a, b, c, sg, sg, sg, wa, wb, wc)


def _wo_kernel(m_ref, w_ref, x_ref, g_ref, o_ref):
    kk = pl.program_id(1)

    @pl.when(kk == 0)
    def _():
        o_ref[...] = jnp.zeros(o_ref.shape, F32)

    o_ref[...] += jnp.dot(m_ref[...], w_ref[...].astype(BF16), preferred_element_type=F32)

    @pl.when(kk == pl.num_programs(1) - 1)
    def _():
        o_ref[...] = x_ref[...] + _rms(o_ref[...], g_ref[...])


def _wo_norm(mm, w, x, g, layer, *, tm, tk):
    m, d = x.shape
    return pl.pallas_call(
        _wo_kernel,
        grid=(m // tm, d // tk),
        in_specs=[pl.BlockSpec((tm, tk), lambda i, k: (i, k)),
                  pl.BlockSpec((None, tk, d), lambda i, k: (layer, k, 0)),
                  pl.BlockSpec((tm, d), lambda i, k: (i, 0)),
                  pl.BlockSpec((None, 1, d), lambda i, k: (layer, 0, 0))],
        out_specs=pl.BlockSpec((tm, d), lambda i, k: (i, 0)),
        out_shape=jax.ShapeDtypeStruct((m, d), F32),
        compiler_params=_cparams("parallel", "arbitrary"),
        name="wo_norm",
    )(mm, w, x, g)


def _ffn_kernel(x_ref, gpre_ref, up_ref, down_ref, gpost_ref, o_ref, hn_ref):
    f = pl.program_id(1)

    @pl.when(f == 0)
    def _():
        hn_ref[...] = _rms(x_ref[...], gpre_ref[...]).astype(BF16)
        o_ref[...] = jnp.zeros(o_ref.shape, F32)

    a = jnp.dot(hn_ref[...], up_ref[...].astype(BF16), preferred_element_type=F32)
    a = jnp.square(jnp.maximum(a, 0.0))
    o_ref[...] += jnp.dot(a.astype(BF16), down_ref[...].astype(BF16), preferred_element_type=F32)

    @pl.when(f == pl.num_programs(1) - 1)
    def _():
        o_ref[...] = x_ref[...] + _rms(o_ref[...], gpost_ref[...])


def _ffn(x, gpre, up, down, gpost, layer, *, tm, tf):
    m, d = x.shape
    dff = up.shape[2]
    vec = pl.BlockSpec((None, 1, d), lambda i, f: (layer, 0, 0))
    return pl.pallas_call(
        _ffn_kernel,
        grid=(m // tm, dff // tf),
        in_specs=[pl.BlockSpec((tm, d), lambda i, f: (i, 0)), vec,
                  pl.BlockSpec((None, d, tf), lambda i, f: (layer, 0, f)),
                  pl.BlockSpec((None, tf, d), lambda i, f: (layer, f, 0)), vec],
        out_specs=pl.BlockSpec((tm, d), lambda i, f: (i, 0)),
        out_shape=jax.ShapeDtypeStruct((m, d), F32),
        scratch_shapes=[pltpu.VMEM((tm, d), BF16)],
        compiler_params=_cparams("parallel", "arbitrary"),
        name="ffn",
    )(x, gpre, up, down, gpost)


def _ple_kernel(x_ref, xc_ref, p_ref, wg_ref, wp_ref, o_ref, xb_ref):
    @pl.when(pl.program_id(1) == 0)
    def _():
        xb_ref[...] = x_ref[...].astype(BF16)

    gate = jax.nn.sigmoid(jnp.dot(xb_ref[...], wg_ref[...].astype(BF16), preferred_element_type=F32))
    proj = _bdot(p_ref[...], wp_ref[...])
    o_ref[...] = xc_ref[...] + gate * proj


def _ple(x, p, wg, wp, layer, *, tm, tn):
    m, d = x.shape
    pd = p.shape[1]
    return pl.pallas_call(
        _ple_kernel,
        grid=(m // tm, d // tn),
        in_specs=[pl.BlockSpec((tm, d), lambda i, j: (i, 0)),
                  pl.BlockSpec((tm, tn), lambda i, j: (i, j)),
                  pl.BlockSpec((tm, pd), lambda i, j: (i, 0)),
                  pl.BlockSpec((None, d, tn), lambda i, j: (layer, 0, j)),
                  pl.BlockSpec((None, pd, tn), lambda i, j: (layer, 0, j))],
        out_specs=pl.BlockSpec((tm, tn), lambda i, j: (i, j)),
        out_shape=jax.ShapeDtypeStruct((m, d), F32),
        scratch_shapes=[pltpu.VMEM((tm, d), BF16)],
        compiler_params=_cparams("parallel", "arbitrary"),
        name="ple",
    )(x, x, p, wg, wp)


def _scan_consts():
    lane = np.arange(LANE)
    ones_bd = (lane[:, None] // A_HEAD == lane[None, :] // A_HEAD).astype(np.float32)
    eye = (lane[None, :] % A_HEAD == np.arange(A_HEAD)[:, None]).astype(np.float32)
    sel = np.zeros((16, LANE), np.float32)
    sel[0, :A_HEAD] = 1.0
    sel[1, A_HEAD:] = 1.0
    return jnp.asarray(ones_bd, BF16), eye, jnp.asarray(sel, BF16)


def _rope_tables(pos, rope):
    inv_freq = ROPE_THETA ** (-jnp.arange(0, rope, 2, dtype=F32) / rope)
    ang = pos.astype(F32)[:, None] * inv_freq[None, :]
    cos, sin = jnp.cos(ang), jnp.sin(ang)
    return jnp.concatenate([cos, cos], axis=-1), jnp.concatenate([sin, sin], axis=-1)


def _rot_half_cols(w):
    half = w.shape[-1] // 2
    return jnp.concatenate([-w[..., half:], w[..., :half]], axis=-1)


def kernel(x_prompt, x_sample, state_rwkv, state_rwkv_shift, cache_ckv, cache_kr, page_table,
           p_prompt, p_sample, norm_mix_pre, norm_mix_post, norm_ffn_pre, norm_ffn_post, w_in,
           rwkv_mu, rwkv_w0, rwkv_w2, rwkv_a0, rwkv_a2, rwkv_g2, rwkv_k_k, rwkv_k_a, rwkv_r_k,
           rwkv_lnx_g, rwkv_lnx_b, w_out_a, mla_q_norm, mla_w_uq, mla_kv_norm, mla_w_uk, mla_w_uv,
           w_out_b, gmlp_ln_g, gmlp_ln_b, gmlp_w_s, gmlp_b_s, w_out_c, w_o, ffn_up, ffn_down,
           ple_proj, ple_gate):
    depth = w_in.shape[0]
    bsz, seq, d = x_prompt.shape
    dbsz, dseq, _ = x_sample.shape
    mp, ms = bsz * seq, dbsz * dseq
    mt = mp + ms
    da = rwkv_w0.shape[1]
    heads_a = da // A_HEAD
    n_a_in = rwkv_mu.shape[1]
    lora = mla_q_norm.shape[1]
    kv_lora = mla_kv_norm.shape[1]
    assert lora == kv_lora
    b_heads, nope = mla_w_uk.shape[2], mla_w_uk.shape[3]
    rope = mla_w_uq.shape[3] - nope
    vdim = mla_w_uv.shape[3]
    dc = gmlp_ln_g.shape[1]
    groups = gmlp_w_s.shape[1]
    past = page_table.shape[1] * PAGE_SIZE
    scale = float((nope + rope) ** -0.5)
    n_b_in = lora + kv_lora + rope
    col_b = n_a_in
    col_c = n_a_in + n_b_in
    col_g = col_c + 2 * dc
    wa_ext = -(-n_a_in // 512) * 512
    assert 3 * da + 512 == wa_ext and col_c % 512 == 0 and col_g % 512 == 0

    def v3(a):
        return a.reshape(a.shape[0], 1, -1)

    kr0 = col_b + lora + kv_lora
    zpad = jnp.zeros((depth, d, LANE - rope), F32)
    w_b = jnp.concatenate([w_in[:, :, col_b:kr0], w_in[:, :, kr0:kr0 + rope], zpad,
                           _rot_half_cols(w_in[:, :, kr0:kr0 + rope]), zpad], axis=-1).astype(BF16)
    w_in = w_in.astype(BF16)
    ffn_up, ffn_down = ffn_up.astype(BF16), ffn_down.astype(BF16)
    w_o, ple_gate = w_o.astype(BF16), ple_gate.astype(BF16)
    w_out_a, w_out_b, w_out_c = w_out_a.astype(BF16), w_out_b.astype(BF16), w_out_c.astype(BF16)
    wl = jnp.zeros((depth, wa_ext - 3 * da, 3 * da), F32)
    wl = wl.at[:, 0:W_LORA, 0:da].set(rwkv_w2)
    wl = wl.at[:, W_LORA:W_LORA + A_LORA, da:2 * da].set(rwkv_a2)
    wl = wl.at[:, W_LORA + A_LORA:W_LORA + A_LORA + G_LORA, 2 * da:3 * da].set(rwkv_g2)
    wl = wl.astype(BF16)
    mu_ext = jnp.pad(rwkv_mu, ((0, 0), (0, wa_ext - n_a_in)))
    rwkv_vecs = {"mu": v3(mu_ext), "w0": v3(rwkv_w0), "a0": v3(rwkv_a0), "k_k": v3(rwkv_k_k),
                 "k_a": v3(rwkv_k_a), "r_k": v3(rwkv_r_k)}
    w_qn = mla_w_uq[..., :nope].reshape(depth, lora, b_heads * nope)
    w_qr = jnp.transpose(mla_w_uq[..., nope:], (0, 2, 1, 3))
    w_qrr = _rot_half_cols(w_qr)
    w_ukt = jnp.transpose(mla_w_uk, (0, 2, 3, 1))
    w_uvt = jnp.transpose(mla_w_uv, (0, 2, 1, 3))
    seqs_per_chunk = CHUNK // dseq
    tri = np.tril(np.ones((CHUNK, CHUNK), np.float32))
    blk = np.kron(np.eye(seqs_per_chunk, dtype=np.float32), np.tril(np.ones((dseq, dseq), np.float32)))
    sp_mask = jnp.asarray(np.stack([tri, blk]))
    ws_s = jnp.tile(gmlp_w_s[:, :, :dseq, :dseq], (1, 1, seqs_per_chunk, seqs_per_chunk))
    ws_all = jnp.stack([gmlp_w_s, ws_s], axis=1)
    gd = dc // groups
    bias_p = jnp.repeat(jnp.swapaxes(gmlp_b_s, 1, 2), gd, axis=2)
    bias_s = jnp.tile(bias_p[:, :dseq], (1, seqs_per_chunk, 1))
    bias_all = jnp.stack([bias_p, bias_s], axis=1)

    cs_p, sn_p = _rope_tables(jnp.arange(seq), rope)
    cs_s, sn_s = _rope_tables(past + jnp.arange(dseq), rope)
    cs = jnp.concatenate([jnp.tile(cs_p, (bsz, 1)), jnp.tile(cs_s, (dbsz, 1))], axis=0)
    sn = jnp.concatenate([jnp.tile(sn_p, (bsz, 1)), jnp.tile(sn_s, (dbsz, 1))], axis=0)
    ones_bd, eye, sel = _scan_consts()
    cache_krt = jnp.swapaxes(cache_kr, 2, 3)

    x = jnp.concatenate([x_prompt.reshape(mp, d), x_sample.reshape(ms, d)], axis=0)
    ple_all = jnp.concatenate([p_prompt.reshape(depth, mp, -1), p_sample.reshape(depth, ms, -1)], axis=1)

    tm_big = _pick_tile(mt, 1024)
    tm_mid = _pick_tile(mt, 512)
    tm_prep_p = _pick_tile(seq, 256, 8)
    tm_prep_s = _pick_tile(ms, 256, 8)
    tm_g = _pick_tile(int(np.gcd(mp, ms)), 512)
    tq = _pick_tile(seq, 128)
    tk = _pick_tile(seq, 512)
    nb_s = 2 if dbsz % 2 == 0 else 1
    npg = max(n for n in (16, 8, 4, 2, 1) if page_table.shape[1] % n == 0)
    nb_uv = _pick_tile(dbsz, 16, 1)

    outs = {k: [] for k in ("st_p", "sh_p", "ckv_p", "kr_p", "st_s", "sh_s", "ckv_s", "kr_s", "vc_s")}
    for i in range(depth):
        g_pre = v3(norm_mix_pre)
        za = _norm_mm(x, g_pre, w_in, i, col0=0, ncols=wa_ext, tn=wa_ext // 4, tm=tm_big)
        zb = _norm_mm(x, g_pre, w_b, i, col0=0, ncols=w_b.shape[2], tn=w_b.shape[2] // 2, tm=tm_big)
        zc = _norm_mm(x, g_pre, w_in, i, col0=col_c, ncols=2 * dc, tn=512, tm=tm_big)
        sg = _norm_mm(x, g_pre, w_in, i, col0=col_g, ncols=3 * d, tn=512, tm=tm_big, act="sigmoid")

        shift0 = jnp.pad(state_rwkv_shift[i], ((0, 0), (0, wa_ext - n_a_in)))
        shift_rows = jnp.repeat(shift0, dseq, axis=0)
        prep_p = _rwkv_prep(za, za, rwkv_vecs, wl, ones_bd, i, row0=0, nrows=mp, tm=tm_prep_p,
                            seq_len=seq, aux_is_carry=True)
        prep_s = _rwkv_prep(za, shift_rows, rwkv_vecs, wl, ones_bd, i, row0=mp, nrows=ms,
                            tm=tm_prep_s, seq_len=dseq, aux_is_carry=False)
        def scan_consts(nb):
            return [ones_bd, jnp.asarray(np.tile(eye, (nb * heads_a // 2, 1))), sel]

        y_p, st_p = _rwkv_scan(None, [a.reshape(bsz, seq, da) for a in prep_p[:6]],
                               scan_consts(bsz), i, nb=bsz, tc=8)
        y_s, st_s = _rwkv_scan(state_rwkv, [a.reshape(dbsz, dseq, da) for a in prep_s[:6]],
                               scan_consts(nb_s), i, nb=nb_s, tc=dseq)
        a_p = _rwkv_post(y_p.reshape(mp, da), prep_p[7], prep_p[6], v3(rwkv_lnx_g), v3(rwkv_lnx_b),
                         ones_bd, i, tm=_pick_tile(mp, 512))
        a_s = _rwkv_post(y_s.reshape(ms, da), prep_s[7], prep_s[6], v3(rwkv_lnx_g), v3(rwkv_lnx_b),
                         ones_bd, i, tm=_pick_tile(ms, 512))
        a_pre = jnp.concatenate([a_p, a_s], axis=0)
        sh_p = za[seq - 1:mp:seq, :n_a_in]
        sh_s = za[mp + dseq - 1::dseq, :n_a_in]

        q_all, k_all, ckv, kr = _mla_proj(zb, cs, sn, v3(mla_q_norm), v3(mla_kv_norm), w_qn, w_qr,
                                          w_qrr, w_ukt, i, tm=tm_mid, lora=lora, rope=rope)
        b_p = _flash_prompt(q_all, k_all, w_uvt, i, bsz=bsz, t=seq, tq=tq, tk=tk, lora=lora,
                            scale=scale)
        q_s = q_all[:, mp:].reshape(b_heads, dbsz, dseq, -1)
        q_s = jnp.transpose(q_s, (1, 0, 2, 3)).reshape(dbsz, b_heads * dseq, -1)
        knew = jnp.pad(k_all[mp:].reshape(dbsz, dseq, -1), ((0, 0), (0, 16 - dseq), (0, 0)))
        o_s = _paged_attend(page_table, q_s, cache_ckv, cache_krt, knew, i, npg=npg, lora=lora,
                            rope=rope, n_new=dseq, scale=scale)
        b_s = _uv_sample(o_s, w_uvt, i, nb=nb_uv, n_new=dseq)
        b_pre = jnp.concatenate([b_p, b_s], axis=0)

        c_pre, v_c = _gmlp(zc, v3(gmlp_ln_g), v3(gmlp_ln_b), ws_all, sp_mask, bias_all, i,
                           tm=tm_g, prompt_tiles=mp // tm_g)

        mm = _merge(a_pre, b_pre, c_pre, sg, w_out_a, w_out_b, w_out_c, i, tm=tm_big, tn=512)
        x = _wo_norm(mm, w_o, x, v3(norm_mix_post), i, tm=tm_big, tk=512)
        x = _ffn(x, v3(norm_ffn_pre), ffn_up, ffn_down, v3(norm_ffn_post), i, tm=tm_big, tf=512)
        x = _ple(x, ple_all[i], ple_gate, ple_proj, i, tm=tm_big, tn=512)

        outs["st_p"].append(st_p)
        outs["sh_p"].append(sh_p)
        outs["ckv_p"].append(ckv[:mp].reshape(bsz, seq, kv_lora))
        outs["kr_p"].append(kr[:mp].reshape(bsz, seq, rope))
        outs["st_s"].append(st_s)
        outs["sh_s"].append(sh_s)
        outs["ckv_s"].append(ckv[mp:].reshape(dbsz, dseq, kv_lora))
        outs["kr_s"].append(kr[mp:].reshape(dbsz, dseq, rope))
        outs["vc_s"].append(v_c[mp:].reshape(dbsz, dseq, dc))

    return (x[:mp].reshape(bsz, seq, d), x[mp:].reshape(dbsz, dseq, d),
            jnp.stack(outs["st_p"]), jnp.stack(outs["sh_p"]), jnp.stack(outs["ckv_p"]),
            jnp.stack(outs["kr_p"]), jnp.stack(outs["st_s"]), jnp.stack(outs["sh_s"]),
            jnp.stack(outs["ckv_s"]), jnp.stack(outs["kr_s"]), jnp.stack(outs["vc_s"]))
```

```python
import functools

import jax
import jax.numpy as jnp
import numpy as np
from jax import lax
from jax.experimental import pallas as pl
from jax.experimental.pallas import tpu as pltpu

F32 = jnp.float32
BF16 = jnp.bfloat16

NORM_EPS = 1e-6
LNX_EPS = 64e-5
LN_EPS = 1e-5
ROPE_THETA = 10000.0
PAGE_SIZE = 128
CHUNK = 128
A_HEAD = 64
W_LORA = 96
A_LORA = 96
G_LORA = 256
LANE = 128
VMEM_LIMIT = 56 * 1024 * 1024


def _cparams(*sem):
    return pltpu.CompilerParams(dimension_semantics=sem, vmem_limit_bytes=VMEM_LIMIT)


def _pick_tile(m, target, quantum=LANE):
    t = min(target, m)
    t -= t % quantum
    while m % t:
        t -= quantum
    return t


def _rms(x, g):
    return x * lax.rsqrt(jnp.mean(x * x, axis=-1, keepdims=True) + NORM_EPS) * g


def _bdot(a, b):
    return jnp.dot(a.astype(BF16), b.astype(BF16), preferred_element_type=F32)


def _seg_sum(x, ones_bd):
    outs = []
    for c in range(x.shape[1] // LANE):
        xb = x[:, c * LANE:(c + 1) * LANE]
        hi = xb.astype(BF16)
        r1 = xb - hi.astype(F32)
        mid = r1.astype(BF16)
        lo = (r1 - mid.astype(F32)).astype(BF16)
        s = (jnp.dot(hi, ones_bd, preferred_element_type=F32)
             + jnp.dot(mid, ones_bd, preferred_element_type=F32)
             + jnp.dot(lo, ones_bd, preferred_element_type=F32))
        outs.append(s)
    return jnp.concatenate(outs, axis=1)


def _norm_mm_kernel(x_ref, g_ref, w_ref, o_ref, xn_ref, *, act):
    @pl.when(pl.program_id(1) == 0)
    def _():
        xn_ref[...] = _rms(x_ref[...], g_ref[...]).astype(BF16)

    y = jnp.dot(xn_ref[...], w_ref[...].astype(BF16), preferred_element_type=F32)
    if act == "sigmoid":
        y = jax.nn.sigmoid(y)
    o_ref[...] = y.astype(o_ref.dtype)


def _norm_mm(x, g, w, layer, *, col0, ncols, tn, tm, act=None):
    m, k = x.shape
    j0 = col0 // tn
    assert col0 % tn == 0 and ncols % tn == 0
    w_spec = pl.BlockSpec((None, k, tn), lambda i, j: (layer, 0, j0 + j))
    return pl.pallas_call(
        functools.partial(_norm_mm_kernel, act=act),
        grid=(m // tm, ncols // tn),
        in_specs=[pl.BlockSpec((tm, k), lambda i, j: (i, 0)),
                  pl.BlockSpec((None, 1, k), lambda i, j: (layer, 0, 0)),
                  w_spec],
        out_specs=pl.BlockSpec((tm, tn), lambda i, j: (i, j)),
        out_shape=jax.ShapeDtypeStruct((m, ncols), F32),
        scratch_shapes=[pltpu.VMEM((tm, k), BF16)],
        compiler_params=_cparams("parallel", "arbitrary"),
        name="norm_mm",
    )(x, g, w)


def _rwkv_prep_kernel(za_ref, aux_ref, mu_ref, wl_ref, w0_ref, a0_ref, kkw_ref, ka_ref, rk_ref,
                      ones_ref, r_ref, w_ref, k_ref, v_ref, nkk_ref, b_ref, gate_ref, bonus_ref,
                      *, seq_tiles, rows_per_seq):
    za = za_ref[...]
    tm = za.shape[0]
    row = lax.broadcasted_iota(jnp.int32, za.shape, 0)
    rolled = pltpu.roll(za, 1, 0)
    if seq_tiles is not None:
        first = jnp.where(pl.program_id(0) % seq_tiles == 0, 0.0, aux_ref[7:8, :])
        prev = jnp.where(row == 0, first, rolled)
    else:
        prev = jnp.where(row % rows_per_seq == 0, aux_ref[...], rolled)
    zs = za + mu_ref[...] * (prev - za)
    da = w0_ref.shape[1]
    r = zs[:, 0:da]
    k = zs[:, da:2 * da]
    v = zs[:, 2 * da:3 * da]
    lr = zs[:, 3 * da:]
    col = lax.broadcasted_iota(jnp.int32, lr.shape, 1)
    lact = jnp.where(col < W_LORA, jnp.tanh(lr),
                     jnp.where(col < W_LORA + A_LORA, lr, jax.nn.sigmoid(lr)))
    lo = jnp.dot(lact.astype(BF16), wl_ref[...], preferred_element_type=F32)
    w_log = -jax.nn.softplus(-(w0_ref[...] + lo[:, 0:da])) - 0.5
    decay = jnp.exp(-jnp.exp(w_log))
    a = jax.nn.sigmoid(a0_ref[...] + lo[:, da:2 * da])
    gate = lo[:, 2 * da:3 * da]
    ones_bd = ones_ref[...]
    kk = k * kkw_ref[...]
    kk = kk * lax.rsqrt(jnp.maximum(_seg_sum(kk * kk, ones_bd), 1e-24))
    k2 = k * (1.0 + (a - 1.0) * ka_ref[...])
    r_ref[...] = r
    w_ref[...] = decay
    k_ref[...] = k2
    v_ref[...] = v
    nkk_ref[...] = -kk
    b_ref[...] = kk * a
    gate_ref[...] = gate
    bonus_ref[...] = _seg_sum(r * k2 * rk_ref[...], ones_bd) * v


def _rwkv_prep(za_all, aux, vecs, wl, ones_bd, layer, *, row0, nrows, tm, seq_len, aux_is_carry):
    wa = za_all.shape[1]
    da = vecs["w0"].shape[-1]
    i0 = row0 // tm
    if aux_is_carry:
        seq_tiles = seq_len // tm
        c0 = row0 // 8
        aux_spec = pl.BlockSpec((8, wa), lambda i: (jnp.maximum(c0 + i * (tm // 8) - 1, 0), 0))
        rows_per_seq = None
    else:
        seq_tiles = None
        rows_per_seq = seq_len
        aux_spec = pl.BlockSpec((tm, wa), lambda i: (i, 0))

    def vec(n):
        return pl.BlockSpec((None, 1, n), lambda i: (layer, 0, 0))

    out_spec = pl.BlockSpec((tm, da), lambda i: (i, 0))
    out_sds = jax.ShapeDtypeStruct((nrows, da), F32)
    out_specs, out_shape = [out_spec] * 8, [out_sds] * 8
    return pl.pallas_call(
        functools.partial(_rwkv_prep_kernel, seq_tiles=seq_tiles, rows_per_seq=rows_per_seq),
        grid=(nrows // tm,),
        in_specs=[pl.BlockSpec((tm, wa), lambda i: (i0 + i, 0)), aux_spec, vec(wa),
                  pl.BlockSpec((None,) + wl.shape[1:], lambda i: (layer, 0, 0)),
                  vec(da), vec(da), vec(da), vec(da), vec(da),
                  pl.BlockSpec(ones_bd.shape, lambda i: (0, 0))],
        out_specs=out_specs,
        out_shape=out_shape,
        compiler_params=_cparams("parallel"),
        name="rwkv_prep",
    )(za_all, aux, vecs["mu"], wl, vecs["w0"], vecs["a0"], vecs["k_k"], vecs["k_a"], vecs["r_k"],
      ones_bd)


def _rwkv_scan_kernel(*refs, nb, zero_init):
    if zero_init:
        s0_ref = None
        (r_ref, w_ref, k_ref, v_ref, nkk_ref, b_ref, ones_ref, eye_ref, sel_ref,
         y_ref, so_ref, st_ref) = refs
    else:
        (s0_ref, r_ref, w_ref, k_ref, v_ref, nkk_ref, b_ref, ones_ref, eye_ref, sel_ref,
         y_ref, so_ref, st_ref) = refs
    ci = pl.program_id(1)
    tc = r_ref.shape[1]
    npair = r_ref.shape[2] // LANE
    nq = nb * npair

    @pl.when(ci == 0)
    def _():
        for bb in range(nb):
            for p in range(npair):
                rows = pl.ds((bb * npair + p) * A_HEAD, A_HEAD)
                if zero_init:
                    st_ref[rows, :] = jnp.zeros((A_HEAD, LANE), F32)
                else:
                    st_ref[rows, :] = jnp.concatenate(
                        [s0_ref[bb, 2 * p], s0_ref[bb, 2 * p + 1]], axis=1)

    ones_bd = ones_ref[...]
    eye = eye_ref[...]
    sel = sel_ref[...]
    low_half = lax.broadcasted_iota(jnp.int32, (1, LANE), 1) < A_HEAD

    def rows_of(ref, t):
        return jnp.concatenate(
            [jnp.broadcast_to(ref[bb, pl.ds(t, 1), pl.ds(p * LANE, LANE)], (A_HEAD, LANE))
             for bb in range(nb) for p in range(npair)], axis=0)

    for t in range(tc):
        s_old = st_ref[...]
        m1 = (s_old * rows_of(nkk_ref, t)).astype(BF16)
        sa = jnp.dot(m1, ones_bd, preferred_element_type=F32)
        d = (eye * rows_of(v_ref, t)).astype(BF16)
        v_bc = jnp.dot(d, ones_bd, preferred_element_type=F32)
        s_new = s_old * rows_of(w_ref, t) + sa * rows_of(b_ref, t) + v_bc * rows_of(k_ref, t)
        st_ref[...] = s_new
        m2 = (s_new * rows_of(r_ref, t)).astype(BF16)
        y16 = lax.dot_general(sel, m2, (((1,), (1,)), ((), ())), preferred_element_type=F32)
        for j in range(nq // 2):
            r0 = y16[0:1, j * LANE:(j + 1) * LANE]
            r1 = y16[1:2, j * LANE:(j + 1) * LANE]
            even = jnp.where(low_half, r0, pltpu.roll(r1, A_HEAD, 1))
            odd = jnp.where(low_half, pltpu.roll(r0, A_HEAD, 1), r1)
            for q, val in ((2 * j, even), (2 * j + 1, odd)):
                y_ref[q // npair, pl.ds(t, 1), pl.ds((q % npair) * LANE, LANE)] = val

    @pl.when(ci == pl.num_programs(1) - 1)
    def _():
        for bb in range(nb):
            for p in range(npair):
                s = st_ref[pl.ds((bb * npair + p) * A_HEAD, A_HEAD), :]
                so_ref[bb, 2 * p] = s[:, :A_HEAD]
                so_ref[bb, 2 * p + 1] = s[:, A_HEAD:]


def _rwkv_scan(s0, seqs, consts, layer, *, nb, tc):
    bsz, t, da = seqs[0].shape
    seq_spec = pl.BlockSpec((nb, tc, da), lambda g, c: (g, c, 0))
    heads = da // A_HEAD
    zero_init = s0 is None
    st_spec = pl.BlockSpec((nb, heads, A_HEAD, A_HEAD), lambda g, c: (g, 0, 0, 0))
    const_specs = [pl.BlockSpec(c.shape, lambda g, c_: (0, 0)) for c in consts]
    s0_spec = pl.BlockSpec((None, nb, heads, A_HEAD, A_HEAD), lambda g, c: (layer, g, 0, 0, 0))
    in_specs = ([] if zero_init else [s0_spec]) + [seq_spec] * 6 + const_specs
    args = ([] if zero_init else [s0]) + list(seqs) + list(consts)
    return pl.pallas_call(
        functools.partial(_rwkv_scan_kernel, nb=nb, zero_init=zero_init),
        grid=(bsz // nb, t // tc),
        in_specs=in_specs,
        out_specs=[seq_spec, st_spec],
        out_shape=[jax.ShapeDtypeStruct(seqs[0].shape, F32),
                   jax.ShapeDtypeStruct((bsz, heads, A_HEAD, A_HEAD), F32)],
        scratch_shapes=[pltpu.VMEM((nb * da // LANE * A_HEAD, LANE), F32)],
        compiler_params=_cparams("parallel", "arbitrary"),
        name="rwkv_scan",
    )(*args)


def _rwkv_post_kernel(y_ref, bonus_ref, gate_ref, g_ref, b_ref, ones_ref, o_ref):
    y = y_ref[...]
    ones_bd = ones_ref[...]
    yc = y - _seg_sum(y, ones_bd) * (1.0 / A_HEAD)
    var = _seg_sum(yc * yc, ones_bd) * (1.0 / A_HEAD)
    yn = yc * lax.rsqrt(var + LNX_EPS) * g_ref[...] + b_ref[...]
    o_ref[...] = ((yn + bonus_ref[...]) * gate_ref[...]).astype(o_ref.dtype)


def _rwkv_post(y, bonus, gate, lnx_g, lnx_b, ones_bd, layer, *, tm):
    m, da = bonus.shape
    row = pl.BlockSpec((tm, da), lambda i: (i, 0))
    y_spec = row
    vec = pl.BlockSpec((None, 1, da), lambda i: (layer, 0, 0))
    return pl.pallas_call(
        _rwkv_post_kernel,
        grid=(m // tm,),
        in_specs=[y_spec, row, row, vec, vec, pl.BlockSpec(ones_bd.shape, lambda i: (0, 0))],
        out_specs=row,
        out_shape=jax.ShapeDtypeStruct((m, da), BF16),
        compiler_params=_cparams("parallel"),
        name="rwkv_post",
    )(y, bonus, gate, lnx_g, lnx_b, ones_bd)


def _mla_proj_kernel(zb_ref, cs_ref, sn_ref, qn_ref, kvn_ref, wn_ref, wr_ref, wrr_ref, wuk_ref,
                     q_ref, kall_ref, ckv_ref, kr_ref, *, lora, rope):
    zb = zb_ref[...]
    cs = cs_ref[...]
    sn = sn_ref[...]
    cq = _rms(zb[:, 0:lora], qn_ref[...]).astype(BF16)
    ckv = _rms(zb[:, lora:2 * lora], kvn_ref[...])
    kr = zb[:, 2 * lora:2 * lora + rope] * cs + zb[:, 2 * lora + LANE:2 * lora + LANE + rope] * sn
    ckv_ref[...] = ckv
    kr_ref[...] = kr
    pad = jnp.zeros((zb.shape[0], LANE - rope), BF16)
    kall_ref[:, 0:lora] = ckv.astype(BF16)
    kall_ref[:, lora:lora + LANE] = jnp.concatenate([kr.astype(BF16), pad], axis=1)
    heads = wuk_ref.shape[0]
    nope = wuk_ref.shape[1]
    qn = jnp.dot(cq, wn_ref[...].astype(BF16), preferred_element_type=F32)
    for h in range(heads):
        q_lat = _bdot(qn[:, h * nope:(h + 1) * nope], wuk_ref[h])
        q_rope = (jnp.dot(cq, wr_ref[h].astype(BF16), preferred_element_type=F32) * cs
                  + jnp.dot(cq, wrr_ref[h].astype(BF16), preferred_element_type=F32) * sn)
        q_ref[h, :, 0:lora] = q_lat.astype(BF16)
        q_ref[h, :, lora:lora + LANE] = jnp.concatenate([q_rope.astype(BF16), pad], axis=1)


def _mla_proj(zb, cs, sn, qn, kvn, wn, wr, wrr, wuk, layer, *, tm, lora, rope):
    m = zb.shape[0]
    heads, nope = wuk.shape[1], wuk.shape[2]
    dq = lora + LANE

    def full(a):
        nd = a.ndim - 1
        return pl.BlockSpec((None,) + a.shape[1:], lambda i: (layer,) + (0,) * nd)

    return pl.pallas_call(
        functools.partial(_mla_proj_kernel, lora=lora, rope=rope),
        grid=(m // tm,),
        in_specs=[pl.BlockSpec((tm, zb.shape[1]), lambda i: (i, 0)),
                  pl.BlockSpec((tm, rope), lambda i: (i, 0)),
                  pl.BlockSpec((tm, rope), lambda i: (i, 0)),
                  full(qn), full(kvn), full(wn), full(wr), full(wrr), full(wuk)],
        out_specs=[pl.BlockSpec((heads, tm, dq), lambda i: (0, i, 0)),
                   pl.BlockSpec((tm, dq), lambda i: (i, 0)),
                   pl.BlockSpec((tm, lora), lambda i: (i, 0)),
                   pl.BlockSpec((tm, rope), lambda i: (i, 0))],
        out_shape=[jax.ShapeDtypeStruct((heads, m, dq), BF16),
                   jax.ShapeDtypeStruct((m, dq), BF16),
                   jax.ShapeDtypeStruct((m, lora), F32),
                   jax.ShapeDtypeStruct((m, rope), F32)],
        compiler_params=_cparams("parallel"),
        name="mla_proj",
    )(zb, cs, sn, qn, kvn, wn, wr, wrr, wuk)


NEG_BIG = -1e30


def _flash_kernel(q_ref, k_ref, wuv_ref, o_ref, m_ref, l_ref, acc_ref, *, tq, tk, lora, scale, nsplit):
    qi = pl.program_id(1)
    heads = q_ref.shape[0]
    rows = heads * tq
    m_ref[...] = jnp.full(m_ref.shape, NEG_BIG, F32)
    l_ref[...] = jnp.zeros(l_ref.shape, F32)
    acc_ref[...] = jnp.zeros(acc_ref.shape, F32)

    def tile(ki, masked):
        k = k_ref[pl.ds(pl.multiple_of(ki * tk, tk), tk), :]
        for part in range(nsplit):
            hs = heads // nsplit
            rs = pl.ds(part * hs * tq, hs * tq)
            q = q_ref[part * hs:(part + 1) * hs].reshape(hs * tq, q_ref.shape[2])
            s = lax.dot_general(q, k, (((1,), (1,)), ((), ())), preferred_element_type=F32) * scale
            if masked:
                qpos = qi * tq + lax.broadcasted_iota(jnp.int32, s.shape, 0) % tq
                kpos = ki * tk + lax.broadcasted_iota(jnp.int32, s.shape, 1)
                s = jnp.where(kpos <= qpos, s, NEG_BIG)
            m_prev = m_ref[rs, :]
            m_new = jnp.maximum(m_prev, jnp.max(s, axis=-1, keepdims=True))
            alpha = jnp.exp(m_prev - m_new)
            p = jnp.exp(s - m_new)
            l_ref[rs, :] = alpha * l_ref[rs, :] + jnp.sum(p, axis=-1, keepdims=True)
            acc_ref[rs, :] = alpha * acc_ref[rs, :] + jnp.dot(p.astype(BF16), k[:, 0:lora],
                                                              preferred_element_type=F32)
            m_ref[rs, :] = m_new

    n_full = (qi * tq) // tk

    def body(ki, carry):
        tile(ki, False)
        return carry

    lax.fori_loop(0, n_full, body, 0)
    tile(n_full, True)

    o = (acc_ref[...] / l_ref[...]).astype(BF16)
    vdim = wuv_ref.shape[2]
    for h in range(heads):
        o_ref[:, h * vdim:(h + 1) * vdim] = jnp.dot(
            o[h * tq:(h + 1) * tq], wuv_ref[h].astype(BF16),
            preferred_element_type=F32).astype(o_ref.dtype)


def _flash_prompt(q, kall, wuv, layer, *, bsz, t, tq, tk, lora, scale):
    heads, _, dq = q.shape
    vdim = wuv.shape[3]
    nq = t // tq
    assert tk % tq == 0 and t % tk == 0
    return pl.pallas_call(
        functools.partial(_flash_kernel, tq=tq, tk=tk, lora=lora, scale=scale,
                          nsplit=2 if heads % 2 == 0 else 1),
        grid=(bsz, nq),
        in_specs=[pl.BlockSpec((heads, tq, dq), lambda b, i: (0, b * nq + i, 0)),
                  pl.BlockSpec((t, dq), lambda b, i: (b, 0)),
                  pl.BlockSpec((None,) + wuv.shape[1:], lambda b, i: (layer, 0, 0, 0))],
        out_specs=pl.BlockSpec((tq, heads * vdim), lambda b, i: (b * nq + i, 0)),
        out_shape=jax.ShapeDtypeStruct((bsz * t, heads * vdim), BF16),
        scratch_shapes=[pltpu.VMEM((heads * tq, 1), F32), pltpu.VMEM((heads * tq, 1), F32),
                        pltpu.VMEM((heads * tq, lora), F32)],
        compiler_params=_cparams("parallel", "arbitrary"),
        name="flash_prompt",
    )(q, kall, wuv)


def _paged_kernel(pt_ref, q_ref, ckv_hbm, krt_hbm, knew_ref, o_ref, cbuf_ref, rbuf_ref, sem_ref,
                  m_ref, l_ref, acc_ref, *, layer, npg, n_chunks, lora, rope, n_new, scale):
    b = pl.program_id(0)
    q = q_ref[0]
    q_lat = q[:, 0:lora]
    q_rope = q[:, lora:lora + rope]

    def slot_of(seq, chunk):
        return (seq * n_chunks + chunk) % 2

    def copies(seq, chunk):
        slot = slot_of(seq, chunk)
        out = []
        for mpg in range(npg):
            page = pt_ref[seq, chunk * npg + mpg]
            out.append(pltpu.make_async_copy(ckv_hbm.at[layer, page], cbuf_ref.at[slot, mpg],
                                             sem_ref.at[0, slot]))
            out.append(pltpu.make_async_copy(krt_hbm.at[layer, page], rbuf_ref.at[slot, mpg],
                                             sem_ref.at[1, slot]))
        return out

    def start(seq, chunk):
        for c in copies(seq, chunk):
            c.start()

    @pl.when(b == 0)
    def _():
        start(0, 0)

    m_ref[...] = jnp.full(m_ref.shape, NEG_BIG, F32)
    l_ref[...] = jnp.zeros(l_ref.shape, F32)
    acc_ref[...] = jnp.zeros(acc_ref.shape, F32)

    def update(s, vals):
        m_prev = m_ref[...]
        m_new = jnp.maximum(m_prev, jnp.max(s, axis=-1, keepdims=True))
        alpha = jnp.exp(m_prev - m_new)
        p = jnp.exp(s - m_new)
        l_ref[...] = alpha * l_ref[...] + jnp.sum(p, axis=-1, keepdims=True)
        acc_ref[...] = alpha * acc_ref[...] + jnp.dot(p.astype(BF16), vals,
                                                      preferred_element_type=F32)
        m_ref[...] = m_new

    for chunk in range(n_chunks):
        if chunk + 1 < n_chunks:
            start(b, chunk + 1)
        else:
            @pl.when(b + 1 < pl.num_programs(0))
            def _():
                start(b + 1, 0)
        for c in copies(b, chunk):
            c.wait()
        slot = slot_of(b, chunk)
        cb = cbuf_ref[slot].reshape(npg * PAGE_SIZE, lora).astype(BF16)
        rb = jnp.concatenate([rbuf_ref[slot, mpg].astype(BF16) for mpg in range(npg)], axis=1)
        s = (lax.dot_general(q_lat, cb, (((1,), (1,)), ((), ())), preferred_element_type=F32)
             + jnp.dot(q_rope, rb, preferred_element_type=F32)) * scale
        update(s, cb)

    kn = knew_ref[0]
    sn = lax.dot_general(q, kn, (((1,), (1,)), ((), ())), preferred_element_type=F32) * scale
    t_q = lax.broadcasted_iota(jnp.int32, sn.shape, 0) % n_new
    t_k = lax.broadcasted_iota(jnp.int32, sn.shape, 1)
    sn = jnp.where(t_k <= t_q, sn, NEG_BIG)
    update(sn, kn[:, 0:lora])
    o_ref[0] = acc_ref[...] / l_ref[...]


def _paged_attend(page_table, q, cache_ckv, cache_krt, knew, layer, *, npg, lora, rope, n_new, scale):
    bsz, rows, dq = q.shape
    n_pages = page_table.shape[1]
    assert n_pages % npg == 0
    grid_spec = pltpu.PrefetchScalarGridSpec(
        num_scalar_prefetch=1,
        grid=(bsz,),
        in_specs=[pl.BlockSpec((1, rows, dq), lambda b, pt: (b, 0, 0)),
                  pl.BlockSpec(memory_space=pl.ANY),
                  pl.BlockSpec(memory_space=pl.ANY),
                  pl.BlockSpec((1,) + knew.shape[1:], lambda b, pt: (b, 0, 0))],
        out_specs=pl.BlockSpec((1, rows, lora), lambda b, pt: (b, 0, 0)),
        scratch_shapes=[pltpu.VMEM((2, npg, PAGE_SIZE, lora), F32),
                        pltpu.VMEM((2, npg, rope, PAGE_SIZE), F32),
                        pltpu.SemaphoreType.DMA((2, 2)),
                        pltpu.VMEM((rows, 1), F32), pltpu.VMEM((rows, 1), F32),
                        pltpu.VMEM((rows, lora), F32)],
    )
    return pl.pallas_call(
        functools.partial(_paged_kernel, layer=layer, npg=npg, n_chunks=n_pages // npg, lora=lora,
                          rope=rope, n_new=n_new, scale=scale),
        grid_spec=grid_spec,
        out_shape=jax.ShapeDtypeStruct((bsz, rows, lora), F32),
        compiler_params=_cparams("arbitrary"),
        name="paged_attend",
    )(page_table, q, cache_ckv, cache_krt, knew)


def _uv_kernel(o_ref, wuv_ref, out_ref, *, n_new):
    heads, _, vdim = wuv_ref.shape
    nb = o_ref.shape[0]
    for h in range(heads):
        x = o_ref[:, h * n_new:(h + 1) * n_new, :].reshape(nb * n_new, o_ref.shape[2])
        out_ref[:, h * vdim:(h + 1) * vdim] = _bdot(x, wuv_ref[h]).astype(out_ref.dtype)


def _uv_sample(o_lat, wuv, layer, *, nb, n_new):
    bsz, rows, lora = o_lat.shape
    heads, vdim = wuv.shape[1], wuv.shape[3]
    return pl.pallas_call(
        functools.partial(_uv_kernel, n_new=n_new),
        grid=(bsz // nb,),
        in_specs=[pl.BlockSpec((nb, rows, lora), lambda i: (i, 0, 0)),
                  pl.BlockSpec((None,) + wuv.shape[1:], lambda i: (layer, 0, 0, 0))],
        out_specs=pl.BlockSpec((nb * n_new, heads * vdim), lambda i: (i, 0)),
        out_shape=jax.ShapeDtypeStruct((bsz * n_new, heads * vdim), BF16),
        compiler_params=_cparams("parallel"),
        name="uv_sample",
    )(o_lat, wuv)


def _gmlp_kernel(zc_ref, g_ref, b_ref, ws_ref, mask_ref, bias_ref, c_ref, v_ref):
    zc = zc_ref[...]
    zc = 0.5 * zc * (1.0 + lax.erf(zc * np.float32(np.sqrt(0.5))))
    dc = zc.shape[1] // 2
    u = zc[:, 0:dc]
    v = zc[:, dc:]
    vc = v - jnp.mean(v, axis=-1, keepdims=True)
    vn = vc * lax.rsqrt(jnp.mean(vc * vc, axis=-1, keepdims=True) + LN_EPS) * g_ref[...] + b_ref[...]
    v_ref[...] = vn
    mask = mask_ref[...]
    vb = vn.astype(BF16)
    groups = ws_ref.shape[0]
    gd = dc // groups
    for g in range(groups):
        cols = slice(g * gd, (g + 1) * gd)
        wm = jnp.where(mask > 0, ws_ref[g], 0.0).astype(BF16)
        for c in range(zc.shape[0] // CHUNK):
            rows = slice(c * CHUNK, (c + 1) * CHUNK)
            s = jnp.dot(wm, vb[rows, cols], preferred_element_type=F32) + bias_ref[:, cols]
            c_ref[rows, cols] = (u[rows, cols] * s).astype(c_ref.dtype)


def _gmlp(zc, ln_g, ln_b, ws, mask, bias, layer, *, tm, prompt_tiles):
    m, d2 = zc.shape
    dc = d2 // 2
    groups = ws.shape[2]

    def grp(i):
        return jnp.where(i < prompt_tiles, 0, 1)

    vec = pl.BlockSpec((None, 1, dc), lambda i: (layer, 0, 0))
    return pl.pallas_call(
        _gmlp_kernel,
        grid=(m // tm,),
        in_specs=[pl.BlockSpec((tm, d2), lambda i: (i, 0)), vec, vec,
                  pl.BlockSpec((None, None, groups, CHUNK, CHUNK), lambda i: (layer, grp(i), 0, 0, 0)),
                  pl.BlockSpec((None, CHUNK, CHUNK), lambda i: (grp(i), 0, 0)),
                  pl.BlockSpec((None, None, CHUNK, dc), lambda i: (layer, grp(i), 0, 0))],
        out_specs=[pl.BlockSpec((tm, dc), lambda i: (i, 0)), pl.BlockSpec((tm, dc), lambda i: (i, 0))],
        out_shape=[jax.ShapeDtypeStruct((m, dc), BF16), jax.ShapeDtypeStruct((m, dc), F32)],
        compiler_params=_cparams("parallel"),
        name="gmlp",
    )(zc, ln_g, ln_b, ws, mask, bias)


def _merge_kernel(a_ref, b_ref, c_ref, ga_ref, gb_ref, gc_ref, wa_ref, wb_ref, wc_ref, o_ref):
    m = (ga_ref[...] * jnp.dot(a_ref[...], wa_ref[...].astype(BF16), preferred_element_type=F32)
         + gb_ref[...] * jnp.dot(b_ref[...], wb_ref[...].astype(BF16), preferred_element_type=F32)
         + gc_ref[...] * jnp.dot(c_ref[...], wc_ref[...].astype(BF16), preferred_element_type=F32))
    o_ref[...] = m.astype(o_ref.dtype)


def _merge(a, b, c, sg, wa, wb, wc, layer, *, tm, tn):
    m, kd = a.shape
    n = wa.shape[2]
    nj = n // tn
    pre = pl.BlockSpec((tm, kd), lambda i, j: (i, 0))

    def gate(s):
        return pl.BlockSpec((tm, tn), lambda i, j: (i, s * nj + j))

    w = pl.BlockSpec((None, kd, tn), lambda i, j: (layer, 0, j))
    return pl.pallas_call(
        _merge_kernel,
        grid=(m // tm, nj),
        in_specs=[pre, pre, pre, gate(0), gate(1), gate(2), w, w, w],
        out_specs=pl.BlockSpec((tm, tn), lambda i, j: (i, j)),
        out_shape=jax.ShapeDtypeStruct((m, n), BF16),
        compiler_params=_cparams("parallel", "arbitrary"),
        name="merge",
    )(a, b, c, sg, sg, sg, wa, wb, wc)


def _wo_kernel(m_ref, w_ref, x_ref, g_ref, o_ref):
    kk = pl.program_id(1)

    @pl.when(kk == 0)
    def _():
        o_ref[...] = jnp.zeros(o_ref.shape, F32)

    o_ref[...] += jnp.dot(m_ref[...], w_ref[...].astype(BF16), preferred_element_type=F32)

    @pl.when(kk == pl.num_programs(1) - 1)
    def _():
        o_ref[...] = x_ref[...] + _rms(o_ref[...], g_ref[...])


def _wo_norm(mm, w, x, g, layer, *, tm, tk):
    m, d = x.shape
    return pl.pallas_call(
        _wo_kernel,
        grid=(m // tm, d // tk),
        in_specs=[pl.BlockSpec((tm, tk), lambda i, k: (i, k)),
                  pl.BlockSpec((None, tk, d), lambda i, k: (layer, k, 0)),
                  pl.BlockSpec((tm, d), lambda i, k: (i, 0)),
                  pl.BlockSpec((None, 1, d), lambda i, k: (layer, 0, 0))],
        out_specs=pl.BlockSpec((tm, d), lambda i, k: (i, 0)),
        out_shape=jax.ShapeDtypeStruct((m, d), F32),
        compiler_params=_cparams("parallel", "arbitrary"),
        name="wo_norm",
    )(mm, w, x, g)


def _ffn_kernel(x_ref, gpre_ref, up_ref, down_ref, gpost_ref, o_ref, hn_ref):
    f = pl.program_id(1)

    @pl.when(f == 0)
    def _():
        hn_ref[...] = _rms(x_ref[...], gpre_ref[...]).astype(BF16)
        o_ref[...] = jnp.zeros(o_ref.shape, F32)

    a = jnp.dot(hn_ref[...], up_ref[...].astype(BF16), preferred_element_type=F32)
    a = jnp.square(jnp.maximum(a, 0.0))
    o_ref[...] += jnp.dot(a.astype(BF16), down_ref[...].astype(BF16), preferred_element_type=F32)

    @pl.when(f == pl.num_programs(1) - 1)
    def _():
        o_ref[...] = x_ref[...] + _rms(o_ref[...], gpost_ref[...])


def _ffn(x, gpre, up, down, gpost, layer, *, tm, tf):
    m, d = x.shape
    dff = up.shape[2]
    vec = pl.BlockSpec((None, 1, d), lambda i, f: (layer, 0, 0))
    return pl.pallas_call(
        _ffn_kernel,
        grid=(m // tm, dff // tf),
        in_specs=[pl.BlockSpec((tm, d), lambda i, f: (i, 0)), vec,
                  pl.BlockSpec((None, d, tf), lambda i, f: (layer, 0, f)),
                  pl.BlockSpec((None, tf, d), lambda i, f: (layer, f, 0)), vec],
        out_specs=pl.BlockSpec((tm, d), lambda i, f: (i, 0)),
        out_shape=jax.ShapeDtypeStruct((m, d), F32),
        scratch_shapes=[pltpu.VMEM((tm, d), BF16)],
        compiler_params=_cparams("parallel", "arbitrary"),
        name="ffn",
    )(x, gpre, up, down, gpost)


def _ple_kernel(x_ref, xc_ref, p_ref, wg_ref, wp_ref, o_ref, xb_ref):
    @pl.when(pl.program_id(1) == 0)
    def _():
        xb_ref[...] = x_ref[...].astype(BF16)

    gate = jax.nn.sigmoid(jnp.dot(xb_ref[...], wg_ref[...].astype(BF16), preferred_element_type=F32))
    proj = _bdot(p_ref[...], wp_ref[...])
    o_ref[...] = xc_ref[...] + gate * proj


def _ple(x, p, wg, wp, layer, *, tm, tn):
    m, d = x.shape
    pd = p.shape[1]
    return pl.pallas_call(
        _ple_kernel,
        grid=(m // tm, d // tn),
        in_specs=[pl.BlockSpec((tm, d), lambda i, j: (i, 0)),
                  pl.BlockSpec((tm, tn), lambda i, j: (i, j)),
                  pl.BlockSpec((tm, pd), lambda i, j: (i, 0)),
                  pl.BlockSpec((None, d, tn), lambda i, j: (layer, 0, j)),
                  pl.BlockSpec((None, pd, tn), lambda i, j: (layer, 0, j))],
        out_specs=pl.BlockSpec((tm, tn), lambda i, j: (i, j)),
        out_shape=jax.ShapeDtypeStruct((m, d), F32),
        scratch_shapes=[pltpu.VMEM((tm, d), BF16)],
        compiler_params=_cparams("parallel", "arbitrary"),
        name="ple",
    )(x, x, p, wg, wp)


def _scan_consts():
    lane = np.arange(LANE)
    ones_bd = (lane[:, None] // A_HEAD == lane[None, :] // A_HEAD).astype(np.float32)
    eye = (lane[None, :] % A_HEAD == np.arange(A_HEAD)[:, None]).astype(np.float32)
    sel = np.zeros((16, LANE), np.float32)
    sel[0, :A_HEAD] = 1.0
    sel[1, A_HEAD:] = 1.0
    return jnp.asarray(ones_bd, BF16), eye, jnp.asarray(sel, BF16)


def _rope_tables(pos, rope):
    inv_freq = ROPE_THETA ** (-jnp.arange(0, rope, 2, dtype=F32) / rope)
    ang = pos.astype(F32)[:, None] * inv_freq[None, :]
    cos, sin = jnp.cos(ang), jnp.sin(ang)
    return jnp.concatenate([cos, cos], axis=-1), jnp.concatenate([sin, sin], axis=-1)


def _rot_half_cols(w):
    half = w.shape[-1] // 2
    return jnp.concatenate([-w[..., half:], w[..., :half]], axis=-1)


def kernel(x_prompt, x_sample, state_rwkv, state_rwkv_shift, cache_ckv, cache_kr, page_table,
           p_prompt, p_sample, norm_mix_pre, norm_mix_post, norm_ffn_pre, norm_ffn_post, w_in,
           rwkv_mu, rwkv_w0, rwkv_w2, rwkv_a0, rwkv_a2, rwkv_g2, rwkv_k_k, rwkv_k_a, rwkv_r_k,
           rwkv_lnx_g, rwkv_lnx_b, w_out_a, mla_q_norm, mla_w_uq, mla_kv_norm, mla_w_uk, mla_w_uv,
           w_out_b, gmlp_ln_g, gmlp_ln_b, gmlp_w_s, gmlp_b_s, w_out_c, w_o, ffn_up, ffn_down,
           ple_proj, ple_gate):
    depth = w_in.shape[0]
    bsz, seq, d = x_prompt.shape
    dbsz, dseq, _ = x_sample.shape
    mp, ms = bsz * seq, dbsz * dseq
    mt = mp + ms
    da = rwkv_w0.shape[1]
    heads_a = da // A_HEAD
    n_a_in = rwkv_mu.shape[1]
    lora = mla_q_norm.shape[1]
    kv_lora = mla_kv_norm.shape[1]
    assert lora == kv_lora
    b_heads, nope = mla_w_uk.shape[2], mla_w_uk.shape[3]
    rope = mla_w_uq.shape[3] - nope
    vdim = mla_w_uv.shape[3]
    dc = gmlp_ln_g.shape[1]
    groups = gmlp_w_s.shape[1]
    past = page_table.shape[1] * PAGE_SIZE
    scale = float((nope + rope) ** -0.5)
    n_b_in = lora + kv_lora + rope
    col_b = n_a_in
    col_c = n_a_in + n_b_in
    col_g = col_c + 2 * dc
    wa_ext = -(-n_a_in // 512) * 512
    assert 3 * da + 512 == wa_ext and col_c % 512 == 0 and col_g % 512 == 0

    def v3(a):
        return a.reshape(a.shape[0], 1, -1)

    kr0 = col_b + lora + kv_lora
    zpad = jnp.zeros((depth, d, LANE - rope), F32)
    w_b = jnp.concatenate([w_in[:, :, col_b:kr0], w_in[:, :, kr0:kr0 + rope], zpad,
                           _rot_half_cols(w_in[:, :, kr0:kr0 + rope]), zpad], axis=-1).astype(BF16)
    w_in = w_in.astype(BF16)
    ffn_up, ffn_down = ffn_up.astype(BF16), ffn_down.astype(BF16)
    w_o, ple_gate = w_o.astype(BF16), ple_gate.astype(BF16)
    w_out_a, w_out_b, w_out_c = w_out_a.astype(BF16), w_out_b.astype(BF16), w_out_c.astype(BF16)
    wl = jnp.zeros((depth, wa_ext - 3 * da, 3 * da), F32)
    wl = wl.at[:, 0:W_LORA, 0:da].set(rwkv_w2)
    wl = wl.at[:, W_LORA:W_LORA + A_LORA, da:2 * da].set(rwkv_a2)
    wl = wl.at[:, W_LORA + A_LORA:W_LORA + A_LORA + G_LORA, 2 * da:3 * da].set(rwkv_g2)
    wl = wl.astype(BF16)
    mu_ext = jnp.pad(rwkv_mu, ((0, 0), (0, wa_ext - n_a_in)))
    rwkv_vecs = {"mu": v3(mu_ext), "w0": v3(rwkv_w0), "a0": v3(rwkv_a0), "k_k": v3(rwkv_k_k),
                 "k_a": v3(rwkv_k_a), "r_k": v3(rwkv_r_k)}
    w_qn = mla_w_uq[..., :nope].reshape(depth, lora, b_heads * nope)
    w_qr = jnp.transpose(mla_w_uq[..., nope:], (0, 2, 1, 3))
    w_qrr = _rot_half_cols(w_qr)
    w_ukt = jnp.transpose(mla_w_uk, (0, 2, 3, 1))
    w_uvt = jnp.transpose(mla_w_uv, (0, 2, 1, 3))
    seqs_per_chunk = CHUNK // dseq
    tri = np.tril(np.ones((CHUNK, CHUNK), np.float32))
    blk = np.kron(np.eye(seqs_per_chunk, dtype=np.float32), np.tril(np.ones((dseq, dseq), np.float32)))
    sp_mask = jnp.asarray(np.stack([tri, blk]))
    ws_s = jnp.tile(gmlp_w_s[:, :, :dseq, :dseq], (1, 1, seqs_per_chunk, seqs_per_chunk))
    ws_all = jnp.stack([gmlp_w_s, ws_s], axis=1)
    gd = dc // groups
    bias_p = jnp.repeat(jnp.swapaxes(gmlp_b_s, 1, 2), gd, axis=2)
    bias_s = jnp.tile(bias_p[:, :dseq], (1, seqs_per_chunk, 1))
    bias_all = jnp.stack([bias_p, bias_s], axis=1)

    cs_p, sn_p = _rope_tables(jnp.arange(seq), rope)
    cs_s, sn_s = _rope_tables(past + jnp.arange(dseq), rope)
    cs = jnp.concatenate([jnp.tile(cs_p, (bsz, 1)), jnp.tile(cs_s, (dbsz, 1))], axis=0)
    sn = jnp.concatenate([jnp.tile(sn_p, (bsz, 1)), jnp.tile(sn_s, (dbsz, 1))], axis=0)
    ones_bd, eye, sel = _scan_consts()
    cache_krt = jnp.swapaxes(cache_kr, 2, 3)

    x = jnp.concatenate([x_prompt.reshape(mp, d), x_sample.reshape(ms, d)], axis=0)
    ple_all = jnp.concatenate([p_prompt.reshape(depth, mp, -1), p_sample.reshape(depth, ms, -1)], axis=1)

    tm_big = _pick_tile(mt, 1024)
    tm_mid = _pick_tile(mt, 512)
    tm_prep_p = _pick_tile(seq, 256, 8)
    tm_prep_s = _pick_tile(ms, 256, 8)
    tm_g = _pick_tile(int(np.gcd(mp, ms)), 512)
    tq = _pick_tile(seq, 128)
    tk = _pick_tile(seq, 512)
    nb_s = 2 if dbsz % 2 == 0 else 1
    npg = max(n for n in (16, 8, 4, 2, 1) if page_table.shape[1] % n == 0)
    nb_uv = _pick_tile(dbsz, 16, 1)

    outs = {k: [] for k in ("st_p", "sh_p", "ckv_p", "kr_p", "st_s", "sh_s", "ckv_s", "kr_s", "vc_s")}
    for i in range(depth):
        g_pre = v3(norm_mix_pre)
        za = _norm_mm(x, g_pre, w_in, i, col0=0, ncols=wa_ext, tn=wa_ext // 4, tm=tm_big)
        zb = _norm_mm(x, g_pre, w_b, i, col0=0, ncols=w_b.shape[2], tn=w_b.shape[2] // 2, tm=tm_big)
        zc = _norm_mm(x, g_pre, w_in, i, col0=col_c, ncols=2 * dc, tn=512, tm=tm_big)
        sg = _norm_mm(x, g_pre, w_in, i, col0=col_g, ncols=3 * d, tn=512, tm=tm_big, act="sigmoid")

        shift0 = jnp.pad(state_rwkv_shift[i], ((0, 0), (0, wa_ext - n_a_in)))
        shift_rows = jnp.repeat(shift0, dseq, axis=0)
        prep_p = _rwkv_prep(za, za, rwkv_vecs, wl, ones_bd, i, row0=0, nrows=mp, tm=tm_prep_p,
                            seq_len=seq, aux_is_carry=True)
        prep_s = _rwkv_prep(za, shift_rows, rwkv_vecs, wl, ones_bd, i, row0=mp, nrows=ms,
                            tm=tm_prep_s, seq_len=dseq, aux_is_carry=False)
        def scan_consts(nb):
            return [ones_bd, jnp.asarray(np.tile(eye, (nb * heads_a // 2, 1))), sel]

        y_p, st_p = _rwkv_scan(None, [a.reshape(bsz, seq, da) for a in prep_p[:6]],
                               scan_consts(bsz), i, nb=bsz, tc=8)
        y_s, st_s = _rwkv_scan(state_rwkv, [a.reshape(dbsz, dseq, da) for a in prep_s[:6]],
                               scan_consts(nb_s), i, nb=nb_s, tc=dseq)
        a_p = _rwkv_post(y_p.reshape(mp, da), prep_p[7], prep_p[6], v3(rwkv_lnx_g), v3(rwkv_lnx_b),
                         ones_bd, i, tm=_pick_tile(mp, 512))
        a_s = _rwkv_post(y_s.reshape(ms, da), prep_s[7], prep_s[6], v3(rwkv_lnx_g), v3(rwkv_lnx_b),
                         ones_bd, i, tm=_pick_tile(ms, 512))
        a_pre = jnp.concatenate([a_p, a_s], axis=0)
        sh_p = jnp.concatenate([za[r:r + 1, :n_a_in] for r in range(seq - 1, mp, seq)], axis=0)
        sh_s = za[mp + dseq - 1::dseq, :n_a_in]

        q_all, k_all, ckv, kr = _mla_proj(zb, cs, sn, v3(mla_q_norm), v3(mla_kv_norm), w_qn, w_qr,
                                          w_qrr, w_ukt, i, tm=tm_mid, lora=lora, rope=rope)
        b_p = _flash_prompt(q_all, k_all, w_uvt, i, bsz=bsz, t=seq, tq=tq, tk=tk, lora=lora,
                            scale=scale)
        q_s = q_all[:, mp:].reshape(b_heads, dbsz, dseq, -1)
        q_s = jnp.transpose(q_s, (1, 0, 2, 3)).reshape(dbsz, b_heads * dseq, -1)
        knew = jnp.pad(k_all[mp:].reshape(dbsz, dseq, -1), ((0, 0), (0, 16 - dseq), (0, 0)))
        o_s = _paged_attend(page_table, q_s, cache_ckv, cache_krt, knew, i, npg=npg, lora=lora,
                            rope=rope, n_new=dseq, scale=scale)
        b_s = _uv_sample(o_s, w_uvt, i, nb=nb_uv, n_new=dseq)
        b_pre = jnp.concatenate([b_p, b_s], axis=0)

        c_pre, v_c = _gmlp(zc, v3(gmlp_ln_g), v3(gmlp_ln_b), ws_all, sp_mask, bias_all, i,
                           tm=tm_g, prompt_tiles=mp // tm_g)

        mm = _merge(a_pre, b_pre, c_pre, sg, w_out_a, w_out_b, w_out_c, i, tm=tm_big, tn=512)
        x = _wo_norm(mm, w_o, x, v3(norm_mix_post), i, tm=tm_big, tk=512)
        x = _ffn(x, v3(norm_ffn_pre), ffn_up, ffn_down, v3(norm_ffn_post), i, tm=tm_big, tf=512)
        x = _ple(x, ple_all[i], ple_gate, ple_proj, i, tm=tm_big, tn=512)

        outs["st_p"].append(st_p)
        outs["sh_p"].append(sh_p)
        outs["ckv_p"].append(ckv[:mp].reshape(bsz, seq, kv_lora))
        outs["kr_p"].append(kr[:mp].reshape(bsz, seq, rope))
        outs["st_s"].append(st_s)
        outs["sh_s"].append(sh_s)
        outs["ckv_s"].append(ckv[mp:].reshape(dbsz, dseq, kv_lora))
        outs["kr_s"].append(kr[mp:].reshape(dbsz, dseq, rope))
        outs["vc_s"].append(v_c[mp:].reshape(dbsz, dseq, dc))

    return (x[:mp].reshape(bsz, seq, d), x[mp:].reshape(dbsz, dseq, d),
            jnp.stack(outs["st_p"]), jnp.stack(outs["sh_p"]), jnp.stack(outs["ckv_p"]),
            jnp.stack(outs["kr_p"]), jnp.stack(outs["st_s"]), jnp.stack(outs["sh_s"]),
            jnp.stack(outs["ckv_s"]), jnp.stack(outs["kr_s"]), jnp.stack(outs["vc_s"]))
```

```python
import functools

import jax
import jax.numpy as jnp
import numpy as np
from jax import lax
from jax.experimental import pallas as pl
from jax.experimental.pallas import tpu as pltpu

F32 = jnp.float32
BF16 = jnp.bfloat16

NORM_EPS = 1e-6
LNX_EPS = 64e-5
LN_EPS = 1e-5
ROPE_THETA = 10000.0
PAGE_SIZE = 128
CHUNK = 128
A_HEAD = 64
W_LORA = 96
A_LORA = 96
G_LORA = 256
LANE = 128
VMEM_LIMIT = 56 * 1024 * 1024


def _cparams(*sem):
    return pltpu.CompilerParams(dimension_semantics=sem, vmem_limit_bytes=VMEM_LIMIT)


def _pick_tile(m, target, quantum=LANE):
    t = min(target, m)
    t -= t % quantum
    while m % t:
        t -= quantum
    return t


def _rms(x, g):
    return x * lax.rsqrt(jnp.mean(x * x, axis=-1, keepdims=True) + NORM_EPS) * g


def _bdot(a, b):
    return jnp.dot(a.astype(BF16), b.astype(BF16), preferred_element_type=F32)


def _seg_sum(x, ones_bd):
    outs = []
    for c in range(x.shape[1] // LANE):
        xb = x[:, c * LANE:(c + 1) * LANE]
        hi = xb.astype(BF16)
        r1 = xb - hi.astype(F32)
        mid = r1.astype(BF16)
        lo = (r1 - mid.astype(F32)).astype(BF16)
        s = (jnp.dot(hi, ones_bd, preferred_element_type=F32)
             + jnp.dot(mid, ones_bd, preferred_element_type=F32)
             + jnp.dot(lo, ones_bd, preferred_element_type=F32))
        outs.append(s)
    return jnp.concatenate(outs, axis=1)


def _norm_mm_kernel(x_ref, g_ref, w_ref, o_ref, h_ref):
    @pl.when(pl.program_id(1) == 0)
    def _():
        h_ref[...] = _rms(x_ref[...], g_ref[...]).astype(BF16)

    o_ref[...] = jnp.dot(h_ref[...], w_ref[...].astype(BF16), preferred_element_type=F32)


def _norm_mm(x, g, w, layer, *, col0, ncols, tn, tm):
    m, k = x.shape
    j0 = col0 // tn
    assert col0 % tn == 0 and ncols % tn == 0
    return pl.pallas_call(
        _norm_mm_kernel,
        grid=(m // tm, ncols // tn),
        in_specs=[pl.BlockSpec((tm, k), lambda i, j: (i, 0)),
                  pl.BlockSpec((None, 1, k), lambda i, j: (layer, 0, 0)),
                  pl.BlockSpec((None, k, tn), lambda i, j: (layer, 0, j0 + j))],
        out_specs=[pl.BlockSpec((tm, tn), lambda i, j: (i, j)),
                   pl.BlockSpec((tm, k), lambda i, j: (i, 0))],
        out_shape=[jax.ShapeDtypeStruct((m, ncols), F32), jax.ShapeDtypeStruct((m, k), BF16)],
        compiler_params=_cparams("parallel", "arbitrary"),
        name="norm_mm",
    )(x, g, w)


def _mm_kernel(h_ref, w_ref, o_ref, *, act):
    y = jnp.dot(h_ref[...], w_ref[...].astype(BF16), preferred_element_type=F32)
    if act == "sigmoid":
        y = jax.nn.sigmoid(y)
    o_ref[...] = y


def _mm(h, w, layer, *, col0, ncols, tn, tm, act=None):
    m, k = h.shape
    j0 = col0 // tn
    assert col0 % tn == 0 and ncols % tn == 0
    return pl.pallas_call(
        functools.partial(_mm_kernel, act=act),
        grid=(m // tm, ncols // tn),
        in_specs=[pl.BlockSpec((tm, k), lambda i, j: (i, 0)),
                  pl.BlockSpec((None, k, tn), lambda i, j: (layer, 0, j0 + j))],
        out_specs=pl.BlockSpec((tm, tn), lambda i, j: (i, j)),
        out_shape=jax.ShapeDtypeStruct((m, ncols), F32),
        compiler_params=_cparams("parallel", "parallel"),
        name="mm",
    )(h, w)


def _rwkv_prep_kernel(za_ref, aux_ref, mu_ref, wl_ref, w0_ref, a0_ref, kkw_ref, ka_ref, rk_ref,
                      ones_ref, r_ref, w_ref, k_ref, v_ref, nkk_ref, b_ref, gate_ref, bonus_ref,
                      *, seq_tiles, rows_per_seq):
    za = za_ref[...]
    tm = za.shape[0]
    row = lax.broadcasted_iota(jnp.int32, za.shape, 0)
    rolled = pltpu.roll(za, 1, 0)
    if seq_tiles is not None:
        first = jnp.where(pl.program_id(0) % seq_tiles == 0, 0.0, aux_ref[7:8, :])
        prev = jnp.where(row == 0, first, rolled)
    else:
        prev = jnp.where(row % rows_per_seq == 0, aux_ref[...], rolled)
    zs = za + mu_ref[...] * (prev - za)
    da = w0_ref.shape[1]
    r = zs[:, 0:da]
    k = zs[:, da:2 * da]
    v = zs[:, 2 * da:3 * da]
    lr = zs[:, 3 * da:]
    col = lax.broadcasted_iota(jnp.int32, lr.shape, 1)
    lact = jnp.where(col < W_LORA, jnp.tanh(lr),
                     jnp.where(col < W_LORA + A_LORA, lr, jax.nn.sigmoid(lr)))
    lo = jnp.dot(lact.astype(BF16), wl_ref[...], preferred_element_type=F32)
    w_log = -jax.nn.softplus(-(w0_ref[...] + lo[:, 0:da])) - 0.5
    decay = jnp.exp(-jnp.exp(w_log))
    a = jax.nn.sigmoid(a0_ref[...] + lo[:, da:2 * da])
    gate = lo[:, 2 * da:3 * da]
    ones_bd = ones_ref[...]
    kk = k * kkw_ref[...]
    kk = kk * lax.rsqrt(jnp.maximum(_seg_sum(kk * kk, ones_bd), 1e-24))
    k2 = k * (1.0 + (a - 1.0) * ka_ref[...])
    r_ref[...] = r
    w_ref[...] = decay
    k_ref[...] = k2
    v_ref[...] = v
    nkk_ref[...] = -kk
    b_ref[...] = kk * a
    gate_ref[...] = gate
    bonus_ref[...] = _seg_sum(r * k2 * rk_ref[...], ones_bd) * v


def _rwkv_prep(za_all, aux, vecs, wl, ones_bd, layer, *, row0, nrows, tm, seq_len, aux_is_carry):
    wa = za_all.shape[1]
    da = vecs["w0"].shape[-1]
    i0 = row0 // tm
    if aux_is_carry:
        seq_tiles = seq_len // tm
        c0 = row0 // 8
        aux_spec = pl.BlockSpec((8, wa), lambda i: (jnp.maximum(c0 + i * (tm // 8) - 1, 0), 0))
        rows_per_seq = None
    else:
        seq_tiles = None
        rows_per_seq = seq_len
        aux_spec = pl.BlockSpec((tm, wa), lambda i: (i, 0))

    def vec(n):
        return pl.BlockSpec((None, 1, n), lambda i: (layer, 0, 0))

    out_spec = pl.BlockSpec((tm, da), lambda i: (i, 0))
    out_sds = jax.ShapeDtypeStruct((nrows, da), F32)
    out_specs, out_shape = [out_spec] * 8, [out_sds] * 8
    return pl.pallas_call(
        functools.partial(_rwkv_prep_kernel, seq_tiles=seq_tiles, rows_per_seq=rows_per_seq),
        grid=(nrows // tm,),
        in_specs=[pl.BlockSpec((tm, wa), lambda i: (i0 + i, 0)), aux_spec, vec(wa),
                  pl.BlockSpec((None,) + wl.shape[1:], lambda i: (layer, 0, 0)),
                  vec(da), vec(da), vec(da), vec(da), vec(da),
                  pl.BlockSpec(ones_bd.shape, lambda i: (0, 0))],
        out_specs=out_specs,
        out_shape=out_shape,
        compiler_params=_cparams("parallel"),
        name="rwkv_prep",
    )(za_all, aux, vecs["mu"], wl, vecs["w0"], vecs["a0"], vecs["k_k"], vecs["k_a"], vecs["r_k"],
      ones_bd)


def _rwkv_scan_kernel(*refs, nb, zero_init):
    if zero_init:
        s0_ref = None
        (r_ref, w_ref, k_ref, v_ref, nkk_ref, b_ref, ones_ref, eye_ref, sel_ref,
         y_ref, so_ref, st_ref) = refs
    else:
        (s0_ref, r_ref, w_ref, k_ref, v_ref, nkk_ref, b_ref, ones_ref, eye_ref, sel_ref,
         y_ref, so_ref, st_ref) = refs
    ci = pl.program_id(1)
    tc = r_ref.shape[1]
    npair = r_ref.shape[2] // LANE
    nq = nb * npair

    @pl.when(ci == 0)
    def _():
        for bb in range(nb):
            for p in range(npair):
                rows = pl.ds((bb * npair + p) * A_HEAD, A_HEAD)
                if zero_init:
                    st_ref[rows, :] = jnp.zeros((A_HEAD, LANE), F32)
                else:
                    st_ref[rows, :] = jnp.concatenate(
                        [s0_ref[bb, 2 * p], s0_ref[bb, 2 * p + 1]], axis=1)

    ones_bd = ones_ref[...]
    eye = eye_ref[...]
    sel = sel_ref[...]
    low_half = lax.broadcasted_iota(jnp.int32, (1, LANE), 1) < A_HEAD

    def rows_of(ref, t):
        return jnp.concatenate(
            [jnp.broadcast_to(ref[bb, pl.ds(t, 1), pl.ds(p * LANE, LANE)], (A_HEAD, LANE))
             for bb in range(nb) for p in range(npair)], axis=0)

    for t in range(tc):
        s_old = st_ref[...]
        m1 = (s_old * rows_of(nkk_ref, t)).astype(BF16)
        sa = jnp.dot(m1, ones_bd, preferred_element_type=F32)
        d = (eye * rows_of(v_ref, t)).astype(BF16)
        v_bc = jnp.dot(d, ones_bd, preferred_element_type=F32)
        s_new = s_old * rows_of(w_ref, t) + sa * rows_of(b_ref, t) + v_bc * rows_of(k_ref, t)
        st_ref[...] = s_new
        m2 = (s_new * rows_of(r_ref, t)).astype(BF16)
        y16 = lax.dot_general(sel, m2, (((1,), (1,)), ((), ())), preferred_element_type=F32)
        for j in range(nq // 2):
            r0 = y16[0:1, j * LANE:(j + 1) * LANE]
            r1 = y16[1:2, j * LANE:(j + 1) * LANE]
            even = jnp.where(low_half, r0, pltpu.roll(r1, A_HEAD, 1))
            odd = jnp.where(low_half, pltpu.roll(r0, A_HEAD, 1), r1)
            for q, val in ((2 * j, even), (2 * j + 1, odd)):
                y_ref[q // npair, pl.ds(t, 1), pl.ds((q % npair) * LANE, LANE)] = val

    @pl.when(ci == pl.num_programs(1) - 1)
    def _():
        for bb in range(nb):
            for p in range(npair):
                s = st_ref[pl.ds((bb * npair + p) * A_HEAD, A_HEAD), :]
                so_ref[bb, 2 * p] = s[:, :A_HEAD]
                so_ref[bb, 2 * p + 1] = s[:, A_HEAD:]


def _rwkv_scan(s0, seqs, consts, layer, *, nb, tc):
    bsz, t, da = seqs[0].shape
    seq_spec = pl.BlockSpec((nb, tc, da), lambda g, c: (g, c, 0))
    heads = da // A_HEAD
    zero_init = s0 is None
    st_spec = pl.BlockSpec((nb, heads, A_HEAD, A_HEAD), lambda g, c: (g, 0, 0, 0))
    const_specs = [pl.BlockSpec(c.shape, lambda g, c_: (0, 0)) for c in consts]
    s0_spec = pl.BlockSpec((None, nb, heads, A_HEAD, A_HEAD), lambda g, c: (layer, g, 0, 0, 0))
    in_specs = ([] if zero_init else [s0_spec]) + [seq_spec] * 6 + const_specs
    args = ([] if zero_init else [s0]) + list(seqs) + list(consts)
    return pl.pallas_call(
        functools.partial(_rwkv_scan_kernel, nb=nb, zero_init=zero_init),
        grid=(bsz // nb, t // tc),
        in_specs=in_specs,
        out_specs=[seq_spec, st_spec],
        out_shape=[jax.ShapeDtypeStruct(seqs[0].shape, F32),
                   jax.ShapeDtypeStruct((bsz, heads, A_HEAD, A_HEAD), F32)],
        scratch_shapes=[pltpu.VMEM((nb * da // LANE * A_HEAD, LANE), F32)],
        compiler_params=_cparams("parallel", "arbitrary"),
        name="rwkv_scan",
    )(*args)


def _rwkv_post_kernel(y_ref, bonus_ref, gate_ref, g_ref, b_ref, ones_ref, o_ref):
    y = y_ref[...]
    ones_bd = ones_ref[...]
    yc = y - _seg_sum(y, ones_bd) * (1.0 / A_HEAD)
    var = _seg_sum(yc * yc, ones_bd) * (1.0 / A_HEAD)
    yn = yc * lax.rsqrt(var + LNX_EPS) * g_ref[...] + b_ref[...]
    o_ref[...] = ((yn + bonus_ref[...]) * gate_ref[...]).astype(o_ref.dtype)


def _rwkv_post(y, bonus, gate, lnx_g, lnx_b, ones_bd, layer, *, tm):
    m, da = bonus.shape
    row = pl.BlockSpec((tm, da), lambda i: (i, 0))
    y_spec = row
    vec = pl.BlockSpec((None, 1, da), lambda i: (layer, 0, 0))
    return pl.pallas_call(
        _rwkv_post_kernel,
        grid=(m // tm,),
        in_specs=[y_spec, row, row, vec, vec, pl.BlockSpec(ones_bd.shape, lambda i: (0, 0))],
        out_specs=row,
        out_shape=jax.ShapeDtypeStruct((m, da), BF16),
        compiler_params=_cparams("parallel"),
        name="rwkv_post",
    )(y, bonus, gate, lnx_g, lnx_b, ones_bd)


def _mla_proj_kernel(zb_ref, cs_ref, sn_ref, qn_ref, kvn_ref, wn_ref, wr_ref, wrr_ref, wuk_ref,
                     q_ref, kall_ref, ckv_ref, kr_ref, *, lora, rope):
    zb = zb_ref[...]
    cs = cs_ref[...]
    sn = sn_ref[...]
    cq = _rms(zb[:, 0:lora], qn_ref[...]).astype(BF16)
    ckv = _rms(zb[:, lora:2 * lora], kvn_ref[...])
    kr = zb[:, 2 * lora:2 * lora + rope] * cs + zb[:, 2 * lora + LANE:2 * lora + LANE + rope] * sn
    ckv_ref[...] = ckv
    kr_ref[...] = kr
    pad = jnp.zeros((zb.shape[0], LANE - rope), BF16)
    kall_ref[:, 0:lora] = ckv.astype(BF16)
    kall_ref[:, lora:lora + LANE] = jnp.concatenate([kr.astype(BF16), pad], axis=1)
    heads = wuk_ref.shape[0]
    nope = wuk_ref.shape[1]
    qn = jnp.dot(cq, wn_ref[...].astype(BF16), preferred_element_type=F32)
    for h in range(heads):
        q_lat = _bdot(qn[:, h * nope:(h + 1) * nope], wuk_ref[h])
        q_rope = (jnp.dot(cq, wr_ref[h].astype(BF16), preferred_element_type=F32) * cs
                  + jnp.dot(cq, wrr_ref[h].astype(BF16), preferred_element_type=F32) * sn)
        q_ref[h, :, 0:lora] = q_lat.astype(BF16)
        q_ref[h, :, lora:lora + LANE] = jnp.concatenate([q_rope.astype(BF16), pad], axis=1)


def _mla_proj(zb, cs, sn, qn, kvn, wn, wr, wrr, wuk, layer, *, tm, lora, rope):
    m = zb.shape[0]
    heads, nope = wuk.shape[1], wuk.shape[2]
    dq = lora + LANE

    def full(a):
        nd = a.ndim - 1
        return pl.BlockSpec((None,) + a.shape[1:], lambda i: (layer,) + (0,) * nd)

    return pl.pallas_call(
        functools.partial(_mla_proj_kernel, lora=lora, rope=rope),
        grid=(m // tm,),
        in_specs=[pl.BlockSpec((tm, zb.shape[1]), lambda i: (i, 0)),
                  pl.BlockSpec((tm, rope), lambda i: (i, 0)),
                  pl.BlockSpec((tm, rope), lambda i: (i, 0)),
                  full(qn), full(kvn), full(wn), full(wr), full(wrr), full(wuk)],
        out_specs=[pl.BlockSpec((heads, tm, dq), lambda i: (0, i, 0)),
                   pl.BlockSpec((tm, dq), lambda i: (i, 0)),
                   pl.BlockSpec((tm, lora), lambda i: (i, 0)),
                   pl.BlockSpec((tm, rope), lambda i: (i, 0))],
        out_shape=[jax.ShapeDtypeStruct((heads, m, dq), BF16),
                   jax.ShapeDtypeStruct((m, dq), BF16),
                   jax.ShapeDtypeStruct((m, lora), F32),
                   jax.ShapeDtypeStruct((m, rope), F32)],
        compiler_params=_cparams("parallel"),
        name="mla_proj",
    )(zb, cs, sn, qn, kvn, wn, wr, wrr, wuk)


NEG_BIG = -1e30
PAGED_SLOTS = 3


def _flash_kernel(q_ref, k_ref, wuv_ref, o_ref, m_ref, l_ref, acc_ref, *, tq, tk, lora, scale, nsplit):
    qi = pl.program_id(1)
    heads = q_ref.shape[0]
    rows = heads * tq
    m_ref[...] = jnp.full(m_ref.shape, NEG_BIG, F32)
    l_ref[...] = jnp.zeros(l_ref.shape, F32)
    acc_ref[...] = jnp.zeros(acc_ref.shape, F32)

    def tile(ki, masked):
        k = k_ref[pl.ds(pl.multiple_of(ki * tk, tk), tk), :]
        for part in range(nsplit):
            hs = heads // nsplit
            rs = pl.ds(part * hs * tq, hs * tq)
            q = q_ref[part * hs:(part + 1) * hs].reshape(hs * tq, q_ref.shape[2])
            s = lax.dot_general(q, k, (((1,), (1,)), ((), ())), preferred_element_type=F32) * scale
            if masked:
                qpos = qi * tq + lax.broadcasted_iota(jnp.int32, s.shape, 0) % tq
                kpos = ki * tk + lax.broadcasted_iota(jnp.int32, s.shape, 1)
                s = jnp.where(kpos <= qpos, s, NEG_BIG)
            m_prev = m_ref[rs, :]
            m_new = jnp.maximum(m_prev, jnp.max(s, axis=-1, keepdims=True))
            alpha = jnp.exp(m_prev - m_new)
            p = jnp.exp(s - m_new)
            l_ref[rs, :] = alpha * l_ref[rs, :] + jnp.sum(p, axis=-1, keepdims=True)
            acc_ref[rs, :] = alpha * acc_ref[rs, :] + jnp.dot(p.astype(BF16), k[:, 0:lora],
                                                              preferred_element_type=F32)
            m_ref[rs, :] = m_new

    n_full = (qi * tq) // tk

    def body(ki, carry):
        tile(ki, False)
        return carry

    lax.fori_loop(0, n_full, body, 0)
    tile(n_full, True)

    o = (acc_ref[...] / l_ref[...]).astype(BF16)
    vdim = wuv_ref.shape[2]
    for h in range(heads):
        o_ref[:, h * vdim:(h + 1) * vdim] = jnp.dot(
            o[h * tq:(h + 1) * tq], wuv_ref[h].astype(BF16),
            preferred_element_type=F32).astype(o_ref.dtype)


def _flash_prompt(q, kall, wuv, layer, *, bsz, t, tq, tk, lora, scale):
    heads, _, dq = q.shape
    vdim = wuv.shape[3]
    nq = t // tq
    assert tk % tq == 0 and t % tk == 0
    return pl.pallas_call(
        functools.partial(_flash_kernel, tq=tq, tk=tk, lora=lora, scale=scale,
                          nsplit=2 if heads % 2 == 0 else 1),
        grid=(bsz, nq),
        in_specs=[pl.BlockSpec((heads, tq, dq), lambda b, i: (0, b * nq + i, 0)),
                  pl.BlockSpec((t, dq), lambda b, i: (b, 0)),
                  pl.BlockSpec((None,) + wuv.shape[1:], lambda b, i: (layer, 0, 0, 0))],
        out_specs=pl.BlockSpec((tq, heads * vdim), lambda b, i: (b * nq + i, 0)),
        out_shape=jax.ShapeDtypeStruct((bsz * t, heads * vdim), BF16),
        scratch_shapes=[pltpu.VMEM((heads * tq, 1), F32), pltpu.VMEM((heads * tq, 1), F32),
                        pltpu.VMEM((heads * tq, lora), F32)],
        compiler_params=_cparams("parallel", "arbitrary"),
        name="flash_prompt",
    )(q, kall, wuv)


def _paged_kernel(pt_ref, q_ref, ckv_hbm, krt_hbm, knew_ref, o_ref, cbuf_ref, rbuf_ref, sem_ref,
                  m_ref, l_ref, acc_ref, *, layer, npg, n_chunks, n_seqs, lora, rope, n_new, scale):
    b = pl.program_id(0)
    q = q_ref[0]
    q_lat = q[:, 0:lora]
    q_rope = q[:, lora:lora + rope]

    n_slots = cbuf_ref.shape[0]
    ahead = n_slots - 1

    def slot_of(seq, chunk):
        return (seq * n_chunks + chunk) % n_slots

    def copies(seq, chunk):
        slot = slot_of(seq, chunk)
        out = []
        for mpg in range(npg):
            page = pt_ref[seq, chunk * npg + mpg]
            out.append(pltpu.make_async_copy(ckv_hbm.at[layer, page], cbuf_ref.at[slot, mpg],
                                             sem_ref.at[0, slot]))
            out.append(pltpu.make_async_copy(krt_hbm.at[layer, page], rbuf_ref.at[slot, mpg],
                                             sem_ref.at[1, slot]))
        return out

    def start_at(seq, offset):
        seq_off, chunk = divmod(offset, n_chunks)
        if isinstance(seq, int):
            if seq + seq_off < n_seqs:
                for c in copies(seq + seq_off, chunk):
                    c.start()
        else:
            @pl.when(seq + seq_off < n_seqs)
            def _():
                for c in copies(seq + seq_off, chunk):
                    c.start()

    @pl.when(b == 0)
    def _():
        for offset in range(ahead):
            start_at(0, offset)

    m_ref[...] = jnp.full(m_ref.shape, NEG_BIG, F32)
    l_ref[...] = jnp.zeros(l_ref.shape, F32)
    acc_ref[...] = jnp.zeros(acc_ref.shape, F32)

    def update(s, vals):
        m_prev = m_ref[...]
        m_new = jnp.maximum(m_prev, jnp.max(s, axis=-1, keepdims=True))
        alpha = jnp.exp(m_prev - m_new)
        p = jnp.exp(s - m_new)
        l_ref[...] = alpha * l_ref[...] + jnp.sum(p, axis=-1, keepdims=True)
        acc_ref[...] = alpha * acc_ref[...] + jnp.dot(p.astype(BF16), vals,
                                                      preferred_element_type=F32)
        m_ref[...] = m_new

    for chunk in range(n_chunks):
        start_at(b, chunk + ahead)
        for c in copies(b, chunk):
            c.wait()
        slot = slot_of(b, chunk)
        cb = cbuf_ref[slot].reshape(npg * PAGE_SIZE, lora).astype(BF16)
        rb = jnp.concatenate([rbuf_ref[slot, mpg].astype(BF16) for mpg in range(npg)], axis=1)
        s = (lax.dot_general(q_lat, cb, (((1,), (1,)), ((), ())), preferred_element_type=F32)
             + jnp.dot(q_rope, rb, preferred_element_type=F32)) * scale
        update(s, cb)

    kn = knew_ref[0]
    sn = lax.dot_general(q, kn, (((1,), (1,)), ((), ())), preferred_element_type=F32) * scale
    t_q = lax.broadcasted_iota(jnp.int32, sn.shape, 0) % n_new
    t_k = lax.broadcasted_iota(jnp.int32, sn.shape, 1)
    sn = jnp.where(t_k <= t_q, sn, NEG_BIG)
    update(sn, kn[:, 0:lora])
    o_ref[0] = acc_ref[...] / l_ref[...]


def _paged_attend(page_table, q, cache_ckv, cache_krt, knew, layer, *, npg, lora, rope, n_new, scale):
    bsz, rows, dq = q.shape
    n_pages = page_table.shape[1]
    assert n_pages % npg == 0
    grid_spec = pltpu.PrefetchScalarGridSpec(
        num_scalar_prefetch=1,
        grid=(bsz,),
        in_specs=[pl.BlockSpec((1, rows, dq), lambda b, pt: (b, 0, 0)),
                  pl.BlockSpec(memory_space=pl.ANY),
                  pl.BlockSpec(memory_space=pl.ANY),
                  pl.BlockSpec((1,) + knew.shape[1:], lambda b, pt: (b, 0, 0))],
        out_specs=pl.BlockSpec((1, rows, lora), lambda b, pt: (b, 0, 0)),
        scratch_shapes=[pltpu.VMEM((PAGED_SLOTS, npg, PAGE_SIZE, lora), F32),
                        pltpu.VMEM((PAGED_SLOTS, npg, rope, PAGE_SIZE), F32),
                        pltpu.SemaphoreType.DMA((2, PAGED_SLOTS)),
                        pltpu.VMEM((rows, 1), F32), pltpu.VMEM((rows, 1), F32),
                        pltpu.VMEM((rows, lora), F32)],
    )
    return pl.pallas_call(
        functools.partial(_paged_kernel, layer=layer, npg=npg, n_chunks=n_pages // npg, n_seqs=bsz,
                          lora=lora,
                          rope=rope, n_new=n_new, scale=scale),
        grid_spec=grid_spec,
        out_shape=jax.ShapeDtypeStruct((bsz, rows, lora), F32),
        compiler_params=_cparams("arbitrary"),
        name="paged_attend",
    )(page_table, q, cache_ckv, cache_krt, knew)


def _uv_kernel(o_ref, wuv_ref, out_ref, *, n_new):
    heads, _, vdim = wuv_ref.shape
    nb = o_ref.shape[0]
    for h in range(heads):
        x = o_ref[:, h * n_new:(h + 1) * n_new, :].reshape(nb * n_new, o_ref.shape[2])
        out_ref[:, h * vdim:(h + 1) * vdim] = _bdot(x, wuv_ref[h]).astype(out_ref.dtype)


def _uv_sample(o_lat, wuv, layer, *, nb, n_new):
    bsz, rows, lora = o_lat.shape
    heads, vdim = wuv.shape[1], wuv.shape[3]
    return pl.pallas_call(
        functools.partial(_uv_kernel, n_new=n_new),
        grid=(bsz // nb,),
        in_specs=[pl.BlockSpec((nb, rows, lora), lambda i: (i, 0, 0)),
                  pl.BlockSpec((None,) + wuv.shape[1:], lambda i: (layer, 0, 0, 0))],
        out_specs=pl.BlockSpec((nb * n_new, heads * vdim), lambda i: (i, 0)),
        out_shape=jax.ShapeDtypeStruct((bsz * n_new, heads * vdim), BF16),
        compiler_params=_cparams("parallel"),
        name="uv_sample",
    )(o_lat, wuv)


def _gmlp_kernel(zc_ref, g_ref, b_ref, ws_ref, mask_ref, bias_ref, c_ref, v_ref):
    zc = zc_ref[...]
    zc = 0.5 * zc * (1.0 + lax.erf(zc * np.float32(np.sqrt(0.5))))
    dc = zc.shape[1] // 2
    u = zc[:, 0:dc]
    v = zc[:, dc:]
    vc = v - jnp.mean(v, axis=-1, keepdims=True)
    vn = vc * lax.rsqrt(jnp.mean(vc * vc, axis=-1, keepdims=True) + LN_EPS) * g_ref[...] + b_ref[...]
    v_ref[...] = vn
    mask = mask_ref[...]
    vb = vn.astype(BF16)
    groups = ws_ref.shape[0]
    gd = dc // groups
    for g in range(groups):
        cols = slice(g * gd, (g + 1) * gd)
        wm = jnp.where(mask > 0, ws_ref[g], 0.0).astype(BF16)
        for c in range(zc.shape[0] // CHUNK):
            rows = slice(c * CHUNK, (c + 1) * CHUNK)
            s = jnp.dot(wm, vb[rows, cols], preferred_element_type=F32) + bias_ref[:, cols]
            c_ref[rows, cols] = (u[rows, cols] * s).astype(c_ref.dtype)


def _gmlp(zc, ln_g, ln_b, ws, mask, bias, layer, *, tm, prompt_tiles):
    m, d2 = zc.shape
    dc = d2 // 2
    groups = ws.shape[2]

    def grp(i):
        return jnp.where(i < prompt_tiles, 0, 1)

    vec = pl.BlockSpec((None, 1, dc), lambda i: (layer, 0, 0))
    return pl.pallas_call(
        _gmlp_kernel,
        grid=(m // tm,),
        in_specs=[pl.BlockSpec((tm, d2), lambda i: (i, 0)), vec, vec,
                  pl.BlockSpec((None, None, groups, CHUNK, CHUNK), lambda i: (layer, grp(i), 0, 0, 0)),
                  pl.BlockSpec((None, CHUNK, CHUNK), lambda i: (grp(i), 0, 0)),
                  pl.BlockSpec((None, None, CHUNK, dc), lambda i: (layer, grp(i), 0, 0))],
        out_specs=[pl.BlockSpec((tm, dc), lambda i: (i, 0)), pl.BlockSpec((tm, dc), lambda i: (i, 0))],
        out_shape=[jax.ShapeDtypeStruct((m, dc), BF16), jax.ShapeDtypeStruct((m, dc), F32)],
        compiler_params=_cparams("parallel"),
        name="gmlp",
    )(zc, ln_g, ln_b, ws, mask, bias)


def _merge_kernel(a_ref, b_ref, c_ref, ga_ref, gb_ref, gc_ref, wa_ref, wb_ref, wc_ref, o_ref):
    m = (ga_ref[...] * jnp.dot(a_ref[...], wa_ref[...].astype(BF16), preferred_element_type=F32)
         + gb_ref[...] * jnp.dot(b_ref[...], wb_ref[...].astype(BF16), preferred_element_type=F32)
         + gc_ref[...] * jnp.dot(c_ref[...], wc_ref[...].astype(BF16), preferred_element_type=F32))
    o_ref[...] = m.astype(o_ref.dtype)


def _merge(a, b, c, sg, wa, wb, wc, layer, *, tm, tn):
    m, kd = a.shape
    n = wa.shape[2]
    nj = n // tn
    pre = pl.BlockSpec((tm, kd), lambda i, j: (i, 0))

    def gate(s):
        return pl.BlockSpec((tm, tn), lambda i, j: (i, s * nj + j))

    w = pl.BlockSpec((None, kd, tn), lambda i, j: (layer, 0, j))
    return pl.pallas_call(
        _merge_kernel,
        grid=(m // tm, nj),
        in_specs=[pre, pre, pre, gate(0), gate(1), gate(2), w, w, w],
        out_specs=pl.BlockSpec((tm, tn), lambda i, j: (i, j)),
        out_shape=jax.ShapeDtypeStruct((m, n), BF16),
        compiler_params=_cparams("parallel", "arbitrary"),
        name="merge",
    )(a, b, c, sg, sg, sg, wa, wb, wc)


def _wo_kernel(m_ref, w_ref, x_ref, g_ref, o_ref):
    kk = pl.program_id(1)

    @pl.when(kk == 0)
    def _():
        o_ref[...] = jnp.zeros(o_ref.shape, F32)

    o_ref[...] += jnp.dot(m_ref[...], w_ref[...].astype(BF16), preferred_element_type=F32)

    @pl.when(kk == pl.num_programs(1) - 1)
    def _():
        o_ref[...] = x_ref[...] + _rms(o_ref[...], g_ref[...])


def _wo_norm(mm, w, x, g, layer, *, tm, tk):
    m, d = x.shape
    return pl.pallas_call(
        _wo_kernel,
        grid=(m // tm, d // tk),
        in_specs=[pl.BlockSpec((tm, tk), lambda i, k: (i, k)),
                  pl.BlockSpec((None, tk, d), lambda i, k: (layer, k, 0)),
                  pl.BlockSpec((tm, d), lambda i, k: (i, 0)),
                  pl.BlockSpec((None, 1, d), lambda i, k: (layer, 0, 0))],
        out_specs=pl.BlockSpec((tm, d), lambda i, k: (i, 0)),
        out_shape=jax.ShapeDtypeStruct((m, d), F32),
        compiler_params=_cparams("parallel", "arbitrary"),
        name="wo_norm",
    )(mm, w, x, g)


def _ffn_kernel(x_ref, gpre_ref, up_ref, down_ref, gpost_ref, o_ref, hn_ref):
    f = pl.program_id(1)

    @pl.when(f == 0)
    def _():
        hn_ref[...] = _rms(x_ref[...], gpre_ref[...]).astype(BF16)
        o_ref[...] = jnp.zeros(o_ref.shape, F32)

    a = jnp.dot(hn_ref[...], up_ref[...].astype(BF16), preferred_element_type=F32)
    a = jnp.square(jnp.maximum(a, 0.0))
    o_ref[...] += jnp.dot(a.astype(BF16), down_ref[...].astype(BF16), preferred_element_type=F32)

    @pl.when(f == pl.num_programs(1) - 1)
    def _():
        o_ref[...] = x_ref[...] + _rms(o_ref[...], gpost_ref[...])


def _ffn(x, gpre, up, down, gpost, layer, *, tm, tf):
    m, d = x.shape
    dff = up.shape[2]
    vec = pl.BlockSpec((None, 1, d), lambda i, f: (layer, 0, 0))
    return pl.pallas_call(
        _ffn_kernel,
        grid=(m // tm, dff // tf),
        in_specs=[pl.BlockSpec((tm, d), lambda i, f: (i, 0)), vec,
                  pl.BlockSpec((None, d, tf), lambda i, f: (layer, 0, f)),
                  pl.BlockSpec((None, tf, d), lambda i, f: (layer, f, 0)), vec],
        out_specs=pl.BlockSpec((tm, d), lambda i, f: (i, 0)),
        out_shape=jax.ShapeDtypeStruct((m, d), F32),
        scratch_shapes=[pltpu.VMEM((tm, d), BF16)],
        compiler_params=_cparams("parallel", "arbitrary"),
        name="ffn",
    )(x, gpre, up, down, gpost)


def _ple_kernel(x_ref, xc_ref, p_ref, wg_ref, wp_ref, o_ref, xb_ref):
    @pl.when(pl.program_id(1) == 0)
    def _():
        xb_ref[...] = x_ref[...].astype(BF16)

    gate = jax.nn.sigmoid(jnp.dot(xb_ref[...], wg_ref[...].astype(BF16), preferred_element_type=F32))
    proj = _bdot(p_ref[...], wp_ref[...])
    o_ref[...] = xc_ref[...] + gate * proj


def _ple(x, p, wg, wp, layer, *, tm, tn):
    m, d = x.shape
    pd = p.shape[1]
    return pl.pallas_call(
        _ple_kernel,
        grid=(m // tm, d // tn),
        in_specs=[pl.BlockSpec((tm, d), lambda i, j: (i, 0)),
                  pl.BlockSpec((tm, tn), lambda i, j: (i, j)),
                  pl.BlockSpec((tm, pd), lambda i, j: (i, 0)),
                  pl.BlockSpec((None, d, tn), lambda i, j: (layer, 0, j)),
                  pl.BlockSpec((None, pd, tn), lambda i, j: (layer, 0, j))],
        out_specs=pl.BlockSpec((tm, tn), lambda i, j: (i, j)),
        out_shape=jax.ShapeDtypeStruct((m, d), F32),
        scratch_shapes=[pltpu.VMEM((tm, d), BF16)],
        compiler_params=_cparams("parallel", "arbitrary"),
        name="ple",
    )(x, x, p, wg, wp)


def _scan_consts():
    lane = np.arange(LANE)
    ones_bd = (lane[:, None] // A_HEAD == lane[None, :] // A_HEAD).astype(np.float32)
    eye = (lane[None, :] % A_HEAD == np.arange(A_HEAD)[:, None]).astype(np.float32)
    sel = np.zeros((16, LANE), np.float32)
    sel[0, :A_HEAD] = 1.0
    sel[1, A_HEAD:] = 1.0
    return jnp.asarray(ones_bd, BF16), eye, jnp.asarray(sel, BF16)


def _rope_tables(pos, rope):
    inv_freq = ROPE_THETA ** (-jnp.arange(0, rope, 2, dtype=F32) / rope)
    ang = pos.astype(F32)[:, None] * inv_freq[None, :]
    cos, sin = jnp.cos(ang), jnp.sin(ang)
    return jnp.concatenate([cos, cos], axis=-1), jnp.concatenate([sin, sin], axis=-1)


def _rot_half_cols(w):
    half = w.shape[-1] // 2
    return jnp.concatenate([-w[..., half:], w[..., :half]], axis=-1)


def kernel(x_prompt, x_sample, state_rwkv, state_rwkv_shift, cache_ckv, cache_kr, page_table,
           p_prompt, p_sample, norm_mix_pre, norm_mix_post, norm_ffn_pre, norm_ffn_post, w_in,
           rwkv_mu, rwkv_w0, rwkv_w2, rwkv_a0, rwkv_a2, rwkv_g2, rwkv_k_k, rwkv_k_a, rwkv_r_k,
           rwkv_lnx_g, rwkv_lnx_b, w_out_a, mla_q_norm, mla_w_uq, mla_kv_norm, mla_w_uk, mla_w_uv,
           w_out_b, gmlp_ln_g, gmlp_ln_b, gmlp_w_s, gmlp_b_s, w_out_c, w_o, ffn_up, ffn_down,
           ple_proj, ple_gate):
    depth = w_in.shape[0]
    bsz, seq, d = x_prompt.shape
    dbsz, dseq, _ = x_sample.shape
    mp, ms = bsz * seq, dbsz * dseq
    mt = mp + ms
    da = rwkv_w0.shape[1]
    heads_a = da // A_HEAD
    n_a_in = rwkv_mu.shape[1]
    lora = mla_q_norm.shape[1]
    kv_lora = mla_kv_norm.shape[1]
    assert lora == kv_lora
    b_heads, nope = mla_w_uk.shape[2], mla_w_uk.shape[3]
    rope = mla_w_uq.shape[3] - nope
    vdim = mla_w_uv.shape[3]
    dc = gmlp_ln_g.shape[1]
    groups = gmlp_w_s.shape[1]
    past = page_table.shape[1] * PAGE_SIZE
    scale = float((nope + rope) ** -0.5)
    n_b_in = lora + kv_lora + rope
    col_b = n_a_in
    col_c = n_a_in + n_b_in
    col_g = col_c + 2 * dc
    wa_ext = -(-n_a_in // 512) * 512
    assert 3 * da + 512 == wa_ext and col_c % 512 == 0 and col_g % 512 == 0

    def v3(a):
        return a.reshape(a.shape[0], 1, -1)

    kr0 = col_b + lora + kv_lora
    zpad = jnp.zeros((depth, d, LANE - rope), F32)
    w_b = jnp.concatenate([w_in[:, :, col_b:kr0], w_in[:, :, kr0:kr0 + rope], zpad,
                           _rot_half_cols(w_in[:, :, kr0:kr0 + rope]), zpad], axis=-1).astype(BF16)
    w_in = w_in.astype(BF16)
    ffn_up, ffn_down = ffn_up.astype(BF16), ffn_down.astype(BF16)
    w_o, ple_gate = w_o.astype(BF16), ple_gate.astype(BF16)
    w_out_a, w_out_b, w_out_c = w_out_a.astype(BF16), w_out_b.astype(BF16), w_out_c.astype(BF16)
    wl = jnp.zeros((depth, wa_ext - 3 * da, 3 * da), F32)
    wl = wl.at[:, 0:W_LORA, 0:da].set(rwkv_w2)
    wl = wl.at[:, W_LORA:W_LORA + A_LORA, da:2 * da].set(rwkv_a2)
    wl = wl.at[:, W_LORA + A_LORA:W_LORA + A_LORA + G_LORA, 2 * da:3 * da].set(rwkv_g2)
    wl = wl.astype(BF16)
    mu_ext = jnp.pad(rwkv_mu, ((0, 0), (0, wa_ext - n_a_in)))
    rwkv_vecs = {"mu": v3(mu_ext), "w0": v3(rwkv_w0), "a0": v3(rwkv_a0), "k_k": v3(rwkv_k_k),
                 "k_a": v3(rwkv_k_a), "r_k": v3(rwkv_r_k)}
    w_qn = mla_w_uq[..., :nope].reshape(depth, lora, b_heads * nope)
    w_qr = jnp.transpose(mla_w_uq[..., nope:], (0, 2, 1, 3))
    w_qrr = _rot_half_cols(w_qr)
    w_ukt = jnp.transpose(mla_w_uk, (0, 2, 3, 1))
    w_uvt = jnp.transpose(mla_w_uv, (0, 2, 1, 3))
    seqs_per_chunk = CHUNK // dseq
    tri = np.tril(np.ones((CHUNK, CHUNK), np.float32))
    blk = np.kron(np.eye(seqs_per_chunk, dtype=np.float32), np.tril(np.ones((dseq, dseq), np.float32)))
    sp_mask = jnp.asarray(np.stack([tri, blk]))
    ws_s = jnp.tile(gmlp_w_s[:, :, :dseq, :dseq], (1, 1, seqs_per_chunk, seqs_per_chunk))
    ws_all = jnp.stack([gmlp_w_s, ws_s], axis=1)
    gd = dc // groups
    bias_p = jnp.repeat(jnp.swapaxes(gmlp_b_s, 1, 2), gd, axis=2)
    bias_s = jnp.tile(bias_p[:, :dseq], (1, seqs_per_chunk, 1))
    bias_all = jnp.stack([bias_p, bias_s], axis=1)

    cs_p, sn_p = _rope_tables(jnp.arange(seq), rope)
    cs_s, sn_s = _rope_tables(past + jnp.arange(dseq), rope)
    cs = jnp.concatenate([jnp.tile(cs_p, (bsz, 1)), jnp.tile(cs_s, (dbsz, 1))], axis=0)
    sn = jnp.concatenate([jnp.tile(sn_p, (bsz, 1)), jnp.tile(sn_s, (dbsz, 1))], axis=0)
    ones_bd, eye, sel = _scan_consts()
    cache_krt = jnp.swapaxes(cache_kr, 2, 3)

    x = jnp.concatenate([x_prompt.reshape(mp, d), x_sample.reshape(ms, d)], axis=0)
    ple_all = jnp.concatenate([p_prompt.reshape(depth, mp, -1), p_sample.reshape(depth, ms, -1)], axis=1)

    tm_big = _pick_tile(mt, 1024)
    tm_mid = _pick_tile(mt, 512)
    tm_prep_p = _pick_tile(seq, 256, 8)
    tm_prep_s = _pick_tile(ms, 256, 8)
    tm_g = _pick_tile(int(np.gcd(mp, ms)), 512)
    tq = _pick_tile(seq, 128)
    tk = _pick_tile(seq, 512)
    nb_s = max(n for n in (4, 2, 1) if dbsz % n == 0)
    npg = max(n for n in (16, 8, 4, 2, 1) if page_table.shape[1] % n == 0)
    nb_uv = _pick_tile(dbsz, 16, 1)

    outs = {k: [] for k in ("st_p", "sh_p", "ckv_p", "kr_p", "st_s", "sh_s", "ckv_s", "kr_s", "vc_s")}
    for i in range(depth):
        g_pre = v3(norm_mix_pre)
        za, h = _norm_mm(x, g_pre, w_in, i, col0=0, ncols=wa_ext, tn=wa_ext // 4, tm=tm_big)
        zb = _mm(h, w_b, i, col0=0, ncols=w_b.shape[2], tn=w_b.shape[2] // 2, tm=tm_big)
        zc = _mm(h, w_in, i, col0=col_c, ncols=2 * dc, tn=512, tm=tm_big)
        sg = _mm(h, w_in, i, col0=col_g, ncols=3 * d, tn=512, tm=tm_big, act="sigmoid")

        shift0 = jnp.pad(state_rwkv_shift[i], ((0, 0), (0, wa_ext - n_a_in)))
        shift_rows = jnp.repeat(shift0, dseq, axis=0)
        prep_p = _rwkv_prep(za, za, rwkv_vecs, wl, ones_bd, i, row0=0, nrows=mp, tm=tm_prep_p,
                            seq_len=seq, aux_is_carry=True)
        prep_s = _rwkv_prep(za, shift_rows, rwkv_vecs, wl, ones_bd, i, row0=mp, nrows=ms,
                            tm=tm_prep_s, seq_len=dseq, aux_is_carry=False)
        def scan_consts(nb):
            return [ones_bd, jnp.asarray(np.tile(eye, (nb * heads_a // 2, 1))), sel]

        y_p, st_p = _rwkv_scan(None, [a.reshape(bsz, seq, da) for a in prep_p[:6]],
                               scan_consts(bsz), i, nb=bsz, tc=8)
        y_s, st_s = _rwkv_scan(state_rwkv, [a.reshape(dbsz, dseq, da) for a in prep_s[:6]],
                               scan_consts(nb_s), i, nb=nb_s, tc=dseq)
        a_p = _rwkv_post(y_p.reshape(mp, da), prep_p[7], prep_p[6], v3(rwkv_lnx_g), v3(rwkv_lnx_b),
                         ones_bd, i, tm=_pick_tile(mp, 512))
        a_s = _rwkv_post(y_s.reshape(ms, da), prep_s[7], prep_s[6], v3(rwkv_lnx_g), v3(rwkv_lnx_b),
                         ones_bd, i, tm=_pick_tile(ms, 512))
        a_pre = jnp.concatenate([a_p, a_s], axis=0)
        sh_p = jnp.concatenate([za[r:r + 1, :n_a_in] for r in range(seq - 1, mp, seq)], axis=0)
        sh_s = za[mp + dseq - 1::dseq, :n_a_in]

        q_all, k_all, ckv, kr = _mla_proj(zb, cs, sn, v3(mla_q_norm), v3(mla_kv_norm), w_qn, w_qr,
                                          w_qrr, w_ukt, i, tm=tm_mid, lora=lora, rope=rope)
        b_p = _flash_prompt(q_all, k_all, w_uvt, i, bsz=bsz, t=seq, tq=tq, tk=tk, lora=lora,
                            scale=scale)
        q_s = q_all[:, mp:].reshape(b_heads, dbsz, dseq, -1)
        q_s = jnp.transpose(q_s, (1, 0, 2, 3)).reshape(dbsz, b_heads * dseq, -1)
        knew = jnp.pad(k_all[mp:].reshape(dbsz, dseq, -1), ((0, 0), (0, 16 - dseq), (0, 0)))
        o_s = _paged_attend(page_table, q_s, cache_ckv, cache_krt, knew, i, npg=npg, lora=lora,
                            rope=rope, n_new=dseq, scale=scale)
        b_s = _uv_sample(o_s, w_uvt, i, nb=nb_uv, n_new=dseq)
        b_pre = jnp.concatenate([b_p, b_s], axis=0)

        c_pre, v_c = _gmlp(zc, v3(gmlp_ln_g), v3(gmlp_ln_b), ws_all, sp_mask, bias_all, i,
                           tm=tm_g, prompt_tiles=mp // tm_g)

        mm = _merge(a_pre, b_pre, c_pre, sg, w_out_a, w_out_b, w_out_c, i, tm=tm_big, tn=512)
        x = _wo_norm(mm, w_o, x, v3(norm_mix_post), i, tm=tm_big, tk=512)
        x = _ffn(x, v3(norm_ffn_pre), ffn_up, ffn_down, v3(norm_ffn_post), i, tm=tm_big, tf=512)
        x = _ple(x, ple_all[i], ple_gate, ple_proj, i, tm=tm_big, tn=512)

        outs["st_p"].append(st_p)
        outs["sh_p"].append(sh_p)
        outs["ckv_p"].append(ckv[:mp].reshape(bsz, seq, kv_lora))
        outs["kr_p"].append(kr[:mp].reshape(bsz, seq, rope))
        outs["st_s"].append(st_s)
        outs["sh_s"].append(sh_s)
        outs["ckv_s"].append(ckv[mp:].reshape(dbsz, dseq, kv_lora))
        outs["kr_s"].append(kr[mp:].reshape(dbsz, dseq, rope))
        outs["vc_s"].append(v_c[mp:].reshape(dbsz, dseq, dc))

    return (x[:mp].reshape(bsz, seq, d), x[mp:].reshape(dbsz, dseq, d),
            jnp.stack(outs["st_p"]), jnp.stack(outs["sh_p"]), jnp.stack(outs["ckv_p"]),
            jnp.stack(outs["kr_p"]), jnp.stack(outs["st_s"]), jnp.stack(outs["sh_s"]),
            jnp.stack(outs["ckv_s"]), jnp.stack(outs["kr_s"]), jnp.stack(outs["vc_s"]))
```

```python
import functools

import jax
import jax.numpy as jnp
import numpy as np
from jax import lax
from jax.experimental import pallas as pl
from jax.experimental.pallas import tpu as pltpu

F32 = jnp.float32
BF16 = jnp.bfloat16

NORM_EPS = 1e-6
LNX_EPS = 64e-5
LN_EPS = 1e-5
ROPE_THETA = 10000.0
PAGE_SIZE = 128
CHUNK = 128
A_HEAD = 64
W_LORA = 96
A_LORA = 96
G_LORA = 256
LANE = 128
VMEM_LIMIT = 56 * 1024 * 1024


def _cparams(*sem):
    return pltpu.CompilerParams(dimension_semantics=sem, vmem_limit_bytes=VMEM_LIMIT)


def _pick_tile(m, target, quantum=LANE):
    t = min(target, m)
    t -= t % quantum
    while m % t:
        t -= quantum
    return t


def _rms(x, g):
    return x * lax.rsqrt(jnp.mean(x * x, axis=-1, keepdims=True) + NORM_EPS) * g


def _bdot(a, b):
    return jnp.dot(a.astype(BF16), b.astype(BF16), preferred_element_type=F32)


def _seg_sum(x, ones_bd):
    outs = []
    for c in range(x.shape[1] // LANE):
        xb = x[:, c * LANE:(c + 1) * LANE]
        hi = xb.astype(BF16)
        r1 = xb - hi.astype(F32)
        mid = r1.astype(BF16)
        lo = (r1 - mid.astype(F32)).astype(BF16)
        s = (jnp.dot(hi, ones_bd, preferred_element_type=F32)
             + jnp.dot(mid, ones_bd, preferred_element_type=F32)
             + jnp.dot(lo, ones_bd, preferred_element_type=F32))
        outs.append(s)
    return jnp.concatenate(outs, axis=1)


def _norm_mm_kernel(x_ref, g_ref, w_ref, o_ref, h_ref):
    @pl.when(pl.program_id(1) == 0)
    def _():
        h_ref[...] = _rms(x_ref[...], g_ref[...]).astype(BF16)

    o_ref[...] = jnp.dot(h_ref[...], w_ref[...].astype(BF16), preferred_element_type=F32)


def _norm_mm(x, g, w, layer, *, col0, ncols, tn, tm):
    m, k = x.shape
    j0 = col0 // tn
    assert col0 % tn == 0 and ncols % tn == 0
    return pl.pallas_call(
        _norm_mm_kernel,
        grid=(m // tm, ncols // tn),
        in_specs=[pl.BlockSpec((tm, k), lambda i, j: (i, 0)),
                  pl.BlockSpec((None, 1, k), lambda i, j: (layer, 0, 0)),
                  pl.BlockSpec((None, k, tn), lambda i, j: (layer, 0, j0 + j))],
        out_specs=[pl.BlockSpec((tm, tn), lambda i, j: (i, j)),
                   pl.BlockSpec((tm, k), lambda i, j: (i, 0))],
        out_shape=[jax.ShapeDtypeStruct((m, ncols), F32), jax.ShapeDtypeStruct((m, k), BF16)],
        compiler_params=_cparams("parallel", "arbitrary"),
        name="norm_mm",
    )(x, g, w)


def _mm_kernel(h_ref, w_ref, o_ref, *, act):
    y = jnp.dot(h_ref[...], w_ref[...].astype(BF16), preferred_element_type=F32)
    if act == "sigmoid":
        y = jax.nn.sigmoid(y)
    o_ref[...] = y


def _mm(h, w, layer, *, col0, ncols, tn, tm, act=None):
    m, k = h.shape
    j0 = col0 // tn
    assert col0 % tn == 0 and ncols % tn == 0
    return pl.pallas_call(
        functools.partial(_mm_kernel, act=act),
        grid=(m // tm, ncols // tn),
        in_specs=[pl.BlockSpec((tm, k), lambda i, j: (i, 0)),
                  pl.BlockSpec((None, k, tn), lambda i, j: (layer, 0, j0 + j))],
        out_specs=pl.BlockSpec((tm, tn), lambda i, j: (i, j)),
        out_shape=jax.ShapeDtypeStruct((m, ncols), F32),
        compiler_params=_cparams("parallel", "parallel"),
        name="mm",
    )(h, w)


def _rwkv_prep_kernel(za_ref, aux_ref, mu_ref, wl_ref, w0_ref, a0_ref, kkw_ref, ka_ref, rk_ref,
                      ones_ref, r_ref, w_ref, k_ref, v_ref, nkk_ref, b_ref, gate_ref, bonus_ref,
                      *, seq_tiles, rows_per_seq):
    za = za_ref[...]
    tm = za.shape[0]
    row = lax.broadcasted_iota(jnp.int32, za.shape, 0)
    rolled = pltpu.roll(za, 1, 0)
    if seq_tiles is not None:
        first = jnp.where(pl.program_id(0) % seq_tiles == 0, 0.0, aux_ref[7:8, :])
        prev = jnp.where(row == 0, first, rolled)
    else:
        prev = jnp.where(row % rows_per_seq == 0, aux_ref[...], rolled)
    zs = za + mu_ref[...] * (prev - za)
    da = w0_ref.shape[1]
    r = zs[:, 0:da]
    k = zs[:, da:2 * da]
    v = zs[:, 2 * da:3 * da]
    lr = zs[:, 3 * da:]
    col = lax.broadcasted_iota(jnp.int32, lr.shape, 1)
    lact = jnp.where(col < W_LORA, jnp.tanh(lr),
                     jnp.where(col < W_LORA + A_LORA, lr, jax.nn.sigmoid(lr)))
    lo = jnp.dot(lact.astype(BF16), wl_ref[...], preferred_element_type=F32)
    w_log = -jax.nn.softplus(-(w0_ref[...] + lo[:, 0:da])) - 0.5
    decay = jnp.exp(-jnp.exp(w_log))
    a = jax.nn.sigmoid(a0_ref[...] + lo[:, da:2 * da])
    gate = lo[:, 2 * da:3 * da]
    ones_bd = ones_ref[...]
    kk = k * kkw_ref[...]
    kk = kk * lax.rsqrt(jnp.maximum(_seg_sum(kk * kk, ones_bd), 1e-24))
    k2 = k * (1.0 + (a - 1.0) * ka_ref[...])
    r_ref[...] = r
    w_ref[...] = decay
    k_ref[...] = k2
    v_ref[...] = v
    nkk_ref[...] = -kk
    b_ref[...] = kk * a
    gate_ref[...] = gate
    bonus_ref[...] = _seg_sum(r * k2 * rk_ref[...], ones_bd) * v


def _rwkv_prep(za_all, aux, vecs, wl, ones_bd, layer, *, row0, nrows, tm, seq_len, aux_is_carry):
    wa = za_all.shape[1]
    da = vecs["w0"].shape[-1]
    i0 = row0 // tm
    if aux_is_carry:
        seq_tiles = seq_len // tm
        c0 = row0 // 8
        aux_spec = pl.BlockSpec((8, wa), lambda i: (jnp.maximum(c0 + i * (tm // 8) - 1, 0), 0))
        rows_per_seq = None
    else:
        seq_tiles = None
        rows_per_seq = seq_len
        aux_spec = pl.BlockSpec((tm, wa), lambda i: (i, 0))

    def vec(n):
        return pl.BlockSpec((None, 1, n), lambda i: (layer, 0, 0))

    out_spec = pl.BlockSpec((tm, da), lambda i: (i, 0))
    out_sds = jax.ShapeDtypeStruct((nrows, da), F32)
    out_specs, out_shape = [out_spec] * 8, [out_sds] * 8
    return pl.pallas_call(
        functools.partial(_rwkv_prep_kernel, seq_tiles=seq_tiles, rows_per_seq=rows_per_seq),
        grid=(nrows // tm,),
        in_specs=[pl.BlockSpec((tm, wa), lambda i: (i0 + i, 0)), aux_spec, vec(wa),
                  pl.BlockSpec((None,) + wl.shape[1:], lambda i: (layer, 0, 0)),
                  vec(da), vec(da), vec(da), vec(da), vec(da),
                  pl.BlockSpec(ones_bd.shape, lambda i: (0, 0))],
        out_specs=out_specs,
        out_shape=out_shape,
        compiler_params=_cparams("parallel"),
        name="rwkv_prep",
    )(za_all, aux, vecs["mu"], wl, vecs["w0"], vecs["a0"], vecs["k_k"], vecs["k_a"], vecs["r_k"],
      ones_bd)


def _rwkv_scan_kernel(*refs, nb, zero_init):
    if zero_init:
        s0_ref = None
        (r_ref, w_ref, k_ref, v_ref, nkk_ref, b_ref, ones_ref, eye_ref, sel_ref,
         y_ref, so_ref, st_ref) = refs
    else:
        (s0_ref, r_ref, w_ref, k_ref, v_ref, nkk_ref, b_ref, ones_ref, eye_ref, sel_ref,
         y_ref, so_ref, st_ref) = refs
    ci = pl.program_id(1)
    tc = r_ref.shape[1]
    npair = r_ref.shape[2] // LANE
    nq = nb * npair

    @pl.when(ci == 0)
    def _():
        for bb in range(nb):
            for p in range(npair):
                rows = pl.ds((bb * npair + p) * A_HEAD, A_HEAD)
                if zero_init:
                    st_ref[rows, :] = jnp.zeros((A_HEAD, LANE), F32)
                else:
                    st_ref[rows, :] = jnp.concatenate(
                        [s0_ref[bb, 2 * p], s0_ref[bb, 2 * p + 1]], axis=1)

    ones_bd = ones_ref[...]
    eye = eye_ref[...]
    sel = sel_ref[...]
    low_half = lax.broadcasted_iota(jnp.int32, (1, LANE), 1) < A_HEAD

    def rows_of(ref, t):
        return jnp.concatenate(
            [jnp.broadcast_to(ref[bb, pl.ds(t, 1), pl.ds(p * LANE, LANE)], (A_HEAD, LANE))
             for bb in range(nb) for p in range(npair)], axis=0)

    for t in range(tc):
        s_old = st_ref[...]
        m1 = (s_old * rows_of(nkk_ref, t)).astype(BF16)
        sa = jnp.dot(m1, ones_bd, preferred_element_type=F32)
        d = (eye * rows_of(v_ref, t)).astype(BF16)
        v_bc = jnp.dot(d, ones_bd, preferred_element_type=F32)
        s_new = s_old * rows_of(w_ref, t) + sa * rows_of(b_ref, t) + v_bc * rows_of(k_ref, t)
        st_ref[...] = s_new
        m2 = (s_new * rows_of(r_ref, t)).astype(BF16)
        y16 = lax.dot_general(sel, m2, (((1,), (1,)), ((), ())), preferred_element_type=F32)
        for j in range(nq // 2):
            r0 = y16[0:1, j * LANE:(j + 1) * LANE]
            r1 = y16[1:2, j * LANE:(j + 1) * LANE]
            even = jnp.where(low_half, r0, pltpu.roll(r1, A_HEAD, 1))
            odd = jnp.where(low_half, pltpu.roll(r0, A_HEAD, 1), r1)
            for q, val in ((2 * j, even), (2 * j + 1, odd)):
                y_ref[q // npair, pl.ds(t, 1), pl.ds((q % npair) * LANE, LANE)] = val

    @pl.when(ci == pl.num_programs(1) - 1)
    def _():
        for bb in range(nb):
            for p in range(npair):
                s = st_ref[pl.ds((bb * npair + p) * A_HEAD, A_HEAD), :]
                so_ref[bb, 2 * p] = s[:, :A_HEAD]
                so_ref[bb, 2 * p + 1] = s[:, A_HEAD:]


def _rwkv_scan(s0, seqs, consts, layer, *, nb, tc):
    bsz, t, da = seqs[0].shape
    seq_spec = pl.BlockSpec((nb, tc, da), lambda g, c: (g, c, 0))
    heads = da // A_HEAD
    zero_init = s0 is None
    st_spec = pl.BlockSpec((nb, heads, A_HEAD, A_HEAD), lambda g, c: (g, 0, 0, 0))
    const_specs = [pl.BlockSpec(c.shape, lambda g, c_: (0, 0)) for c in consts]
    s0_spec = pl.BlockSpec((None, nb, heads, A_HEAD, A_HEAD), lambda g, c: (layer, g, 0, 0, 0))
    in_specs = ([] if zero_init else [s0_spec]) + [seq_spec] * 6 + const_specs
    args = ([] if zero_init else [s0]) + list(seqs) + list(consts)
    return pl.pallas_call(
        functools.partial(_rwkv_scan_kernel, nb=nb, zero_init=zero_init),
        grid=(bsz // nb, t // tc),
        in_specs=in_specs,
        out_specs=[seq_spec, st_spec],
        out_shape=[jax.ShapeDtypeStruct(seqs[0].shape, F32),
                   jax.ShapeDtypeStruct((bsz, heads, A_HEAD, A_HEAD), F32)],
        scratch_shapes=[pltpu.VMEM((nb * da // LANE * A_HEAD, LANE), F32)],
        compiler_params=_cparams("parallel", "arbitrary"),
        name="rwkv_scan",
    )(*args)


def _rwkv_post_kernel(y_ref, bonus_ref, gate_ref, g_ref, b_ref, ones_ref, o_ref):
    y = y_ref[...]
    ones_bd = ones_ref[...]
    yc = y - _seg_sum(y, ones_bd) * (1.0 / A_HEAD)
    var = _seg_sum(yc * yc, ones_bd) * (1.0 / A_HEAD)
    yn = yc * lax.rsqrt(var + LNX_EPS) * g_ref[...] + b_ref[...]
    o_ref[...] = ((yn + bonus_ref[...]) * gate_ref[...]).astype(o_ref.dtype)


def _rwkv_post(y, bonus, gate, lnx_g, lnx_b, ones_bd, layer, *, tm):
    m, da = bonus.shape
    row = pl.BlockSpec((tm, da), lambda i: (i, 0))
    y_spec = row
    vec = pl.BlockSpec((None, 1, da), lambda i: (layer, 0, 0))
    return pl.pallas_call(
        _rwkv_post_kernel,
        grid=(m // tm,),
        in_specs=[y_spec, row, row, vec, vec, pl.BlockSpec(ones_bd.shape, lambda i: (0, 0))],
        out_specs=row,
        out_shape=jax.ShapeDtypeStruct((m, da), BF16),
        compiler_params=_cparams("parallel"),
        name="rwkv_post",
    )(y, bonus, gate, lnx_g, lnx_b, ones_bd)


def _mla_proj_kernel(zb_ref, cs_ref, sn_ref, qn_ref, kvn_ref, wn_ref, wr_ref, wrr_ref, wuk_ref,
                     q_ref, kall_ref, ckv_ref, kr_ref, *, lora, rope):
    zb = zb_ref[...]
    cs = cs_ref[...]
    sn = sn_ref[...]
    cq = _rms(zb[:, 0:lora], qn_ref[...]).astype(BF16)
    ckv = _rms(zb[:, lora:2 * lora], kvn_ref[...])
    kr = zb[:, 2 * lora:2 * lora + rope] * cs + zb[:, 2 * lora + LANE:2 * lora + LANE + rope] * sn
    ckv_ref[...] = ckv
    kr_ref[...] = kr
    pad = jnp.zeros((zb.shape[0], LANE - rope), BF16)
    kall_ref[:, 0:lora] = ckv.astype(BF16)
    kall_ref[:, lora:lora + LANE] = jnp.concatenate([kr.astype(BF16), pad], axis=1)
    heads = wuk_ref.shape[0]
    nope = wuk_ref.shape[1]
    qn = jnp.dot(cq, wn_ref[...].astype(BF16), preferred_element_type=F32)
    for h in range(heads):
        q_lat = _bdot(qn[:, h * nope:(h + 1) * nope], wuk_ref[h])
        q_rope = (jnp.dot(cq, wr_ref[h].astype(BF16), preferred_element_type=F32) * cs
                  + jnp.dot(cq, wrr_ref[h].astype(BF16), preferred_element_type=F32) * sn)
        q_ref[h, :, 0:lora] = q_lat.astype(BF16)
        q_ref[h, :, lora:lora + LANE] = jnp.concatenate([q_rope.astype(BF16), pad], axis=1)


def _mla_proj(zb, cs, sn, qn, kvn, wn, wr, wrr, wuk, layer, *, tm, lora, rope):
    m = zb.shape[0]
    heads, nope = wuk.shape[1], wuk.shape[2]
    dq = lora + LANE

    def full(a):
        nd = a.ndim - 1
        return pl.BlockSpec((None,) + a.shape[1:], lambda i: (layer,) + (0,) * nd)

    return pl.pallas_call(
        functools.partial(_mla_proj_kernel, lora=lora, rope=rope),
        grid=(m // tm,),
        in_specs=[pl.BlockSpec((tm, zb.shape[1]), lambda i: (i, 0)),
                  pl.BlockSpec((tm, rope), lambda i: (i, 0)),
                  pl.BlockSpec((tm, rope), lambda i: (i, 0)),
                  full(qn), full(kvn), full(wn), full(wr), full(wrr), full(wuk)],
        out_specs=[pl.BlockSpec((heads, tm, dq), lambda i: (0, i, 0)),
                   pl.BlockSpec((tm, dq), lambda i: (i, 0)),
                   pl.BlockSpec((tm, lora), lambda i: (i, 0)),
                   pl.BlockSpec((tm, rope), lambda i: (i, 0))],
        out_shape=[jax.ShapeDtypeStruct((heads, m, dq), BF16),
                   jax.ShapeDtypeStruct((m, dq), BF16),
                   jax.ShapeDtypeStruct((m, lora), F32),
                   jax.ShapeDtypeStruct((m, rope), F32)],
        compiler_params=_cparams("parallel"),
        name="mla_proj",
    )(zb, cs, sn, qn, kvn, wn, wr, wrr, wuk)


NEG_BIG = -1e30
PAGED_SLOTS = 4


def _flash_kernel(q_ref, k_ref, wuv_ref, o_ref, m_ref, l_ref, acc_ref, *, tq, tk, lora, scale, nsplit):
    qi = pl.program_id(1)
    heads = q_ref.shape[0]
    rows = heads * tq
    m_ref[...] = jnp.full(m_ref.shape, NEG_BIG, F32)
    l_ref[...] = jnp.zeros(l_ref.shape, F32)
    acc_ref[...] = jnp.zeros(acc_ref.shape, F32)

    def tile(ki, masked):
        k = k_ref[pl.ds(pl.multiple_of(ki * tk, tk), tk), :]
        for part in range(nsplit):
            hs = heads // nsplit
            rs = pl.ds(part * hs * tq, hs * tq)
            q = q_ref[part * hs:(part + 1) * hs].reshape(hs * tq, q_ref.shape[2])
            s = lax.dot_general(q, k, (((1,), (1,)), ((), ())), preferred_element_type=F32) * scale
            if masked:
                qpos = qi * tq + lax.broadcasted_iota(jnp.int32, s.shape, 0) % tq
                kpos = ki * tk + lax.broadcasted_iota(jnp.int32, s.shape, 1)
                s = jnp.where(kpos <= qpos, s, NEG_BIG)
            m_prev = m_ref[rs, :]
            m_new = jnp.maximum(m_prev, jnp.max(s, axis=-1, keepdims=True))
            alpha = jnp.exp(m_prev - m_new)
            p = jnp.exp(s - m_new)
            l_ref[rs, :] = alpha * l_ref[rs, :] + jnp.sum(p, axis=-1, keepdims=True)
            acc_ref[rs, :] = alpha * acc_ref[rs, :] + jnp.dot(p.astype(BF16), k[:, 0:lora],
                                                              preferred_element_type=F32)
            m_ref[rs, :] = m_new

    n_full = (qi * tq) // tk

    def body(ki, carry):
        tile(ki, False)
        return carry

    lax.fori_loop(0, n_full, body, 0)
    tile(n_full, True)

    o = (acc_ref[...] / l_ref[...]).astype(BF16)
    vdim = wuv_ref.shape[2]
    for h in range(heads):
        o_ref[:, h * vdim:(h + 1) * vdim] = jnp.dot(
            o[h * tq:(h + 1) * tq], wuv_ref[h].astype(BF16),
            preferred_element_type=F32).astype(o_ref.dtype)


def _flash_prompt(q, kall, wuv, layer, *, bsz, t, tq, tk, lora, scale):
    heads, _, dq = q.shape
    vdim = wuv.shape[3]
    nq = t // tq
    assert tk % tq == 0 and t % tk == 0
    return pl.pallas_call(
        functools.partial(_flash_kernel, tq=tq, tk=tk, lora=lora, scale=scale,
                          nsplit=2 if heads % 2 == 0 else 1),
        grid=(bsz, nq),
        in_specs=[pl.BlockSpec((heads, tq, dq), lambda b, i: (0, b * nq + i, 0)),
                  pl.BlockSpec((t, dq), lambda b, i: (b, 0)),
                  pl.BlockSpec((None,) + wuv.shape[1:], lambda b, i: (layer, 0, 0, 0))],
        out_specs=pl.BlockSpec((tq, heads * vdim), lambda b, i: (b * nq + i, 0)),
        out_shape=jax.ShapeDtypeStruct((bsz * t, heads * vdim), BF16),
        scratch_shapes=[pltpu.VMEM((heads * tq, 1), F32), pltpu.VMEM((heads * tq, 1), F32),
                        pltpu.VMEM((heads * tq, lora), F32)],
        compiler_params=_cparams("parallel", "arbitrary"),
        name="flash_prompt",
    )(q, kall, wuv)


def _paged_kernel(pt_ref, q_ref, ckv_hbm, krt_hbm, knew_ref, o_ref, cbuf_ref, rbuf_ref, sem_ref,
                  c16_ref, s_ref, m_ref, l_ref, acc_ref, *, layer, npg, n_chunks, n_seqs, lora, rope,
                  n_new, scale):
    b = pl.program_id(0)
    q = q_ref[0]
    q_lat = q[:, 0:lora]
    q_rope = q[:, lora:lora + rope]

    n_slots = cbuf_ref.shape[0]
    ahead = n_slots - 1

    def slot_of(seq, chunk):
        return (seq * n_chunks + chunk) % n_slots

    def copies(seq, chunk):
        slot = slot_of(seq, chunk)
        out = []
        for mpg in range(npg):
            page = pt_ref[seq, chunk * npg + mpg]
            out.append(pltpu.make_async_copy(ckv_hbm.at[layer, page], cbuf_ref.at[slot, mpg],
                                             sem_ref.at[0, slot]))
            out.append(pltpu.make_async_copy(krt_hbm.at[layer, page], rbuf_ref.at[slot, mpg],
                                             sem_ref.at[1, slot]))
        return out

    def start_at(seq, offset):
        seq_off, chunk = divmod(offset, n_chunks)
        if isinstance(seq, int):
            if seq + seq_off < n_seqs:
                for c in copies(seq + seq_off, chunk):
                    c.start()
        else:
            @pl.when(seq + seq_off < n_seqs)
            def _():
                for c in copies(seq + seq_off, chunk):
                    c.start()

    @pl.when(b == 0)
    def _():
        for offset in range(ahead):
            start_at(0, offset)

    m_ref[...] = jnp.full(m_ref.shape, NEG_BIG, F32)
    l_ref[...] = jnp.zeros(l_ref.shape, F32)
    acc_ref[...] = jnp.zeros(acc_ref.shape, F32)

    def update(s, vals):
        m_prev = m_ref[...]
        m_new = jnp.maximum(m_prev, jnp.max(s, axis=-1, keepdims=True))
        alpha = jnp.exp(m_prev - m_new)
        p = jnp.exp(s - m_new)
        l_ref[...] = alpha * l_ref[...] + jnp.sum(p, axis=-1, keepdims=True)
        acc_ref[...] = alpha * acc_ref[...] + jnp.dot(p.astype(BF16), vals,
                                                      preferred_element_type=F32)
        m_ref[...] = m_new

    def scores(chunk):
        for c in copies(b, chunk):
            c.wait()
        slot = slot_of(b, chunk)
        cb = cbuf_ref[slot].reshape(npg * PAGE_SIZE, lora).astype(BF16)
        rb = jnp.concatenate([rbuf_ref[slot, mpg].astype(BF16) for mpg in range(npg)], axis=1)
        c16_ref[chunk % 2] = cb
        s_ref[chunk % 2] = (
            lax.dot_general(q_lat, cb, (((1,), (1,)), ((), ())), preferred_element_type=F32)
            + jnp.dot(q_rope, rb, preferred_element_type=F32)) * scale

    scores(0)
    for chunk in range(n_chunks):
        start_at(b, chunk + ahead)
        if chunk + 1 < n_chunks:
            scores(chunk + 1)
        update(s_ref[chunk % 2], c16_ref[chunk % 2])

    kn = knew_ref[0]
    sn = lax.dot_general(q, kn, (((1,), (1,)), ((), ())), preferred_element_type=F32) * scale
    t_q = lax.broadcasted_iota(jnp.int32, sn.shape, 0) % n_new
    t_k = lax.broadcasted_iota(jnp.int32, sn.shape, 1)
    sn = jnp.where(t_k <= t_q, sn, NEG_BIG)
    update(sn, kn[:, 0:lora])
    o_ref[0] = acc_ref[...] / l_ref[...]


def _paged_attend(page_table, q, cache_ckv, cache_krt, knew, layer, *, npg, lora, rope, n_new, scale):
    bsz, rows, dq = q.shape
    n_pages = page_table.shape[1]
    assert n_pages % npg == 0
    grid_spec = pltpu.PrefetchScalarGridSpec(
        num_scalar_prefetch=1,
        grid=(bsz,),
        in_specs=[pl.BlockSpec((1, rows, dq), lambda b, pt: (b, 0, 0)),
                  pl.BlockSpec(memory_space=pl.ANY),
                  pl.BlockSpec(memory_space=pl.ANY),
                  pl.BlockSpec((1,) + knew.shape[1:], lambda b, pt: (b, 0, 0))],
        out_specs=pl.BlockSpec((1, rows, lora), lambda b, pt: (b, 0, 0)),
        scratch_shapes=[pltpu.VMEM((PAGED_SLOTS, npg, PAGE_SIZE, lora), F32),
                        pltpu.VMEM((PAGED_SLOTS, npg, rope, PAGE_SIZE), F32),
                        pltpu.SemaphoreType.DMA((2, PAGED_SLOTS)),
                        pltpu.VMEM((2, npg * PAGE_SIZE, lora), BF16),
                        pltpu.VMEM((2, rows, npg * PAGE_SIZE), F32),
                        pltpu.VMEM((rows, 1), F32), pltpu.VMEM((rows, 1), F32),
                        pltpu.VMEM((rows, lora), F32)],
    )
    return pl.pallas_call(
        functools.partial(_paged_kernel, layer=layer, npg=npg, n_chunks=n_pages // npg, n_seqs=bsz,
                          lora=lora,
                          rope=rope, n_new=n_new, scale=scale),
        grid_spec=grid_spec,
        out_shape=jax.ShapeDtypeStruct((bsz, rows, lora), F32),
        compiler_params=_cparams("arbitrary"),
        name="paged_attend",
    )(page_table, q, cache_ckv, cache_krt, knew)


def _uv_kernel(o_ref, wuv_ref, out_ref, *, n_new):
    heads, _, vdim = wuv_ref.shape
    nb = o_ref.shape[0]
    for h in range(heads):
        x = o_ref[:, h * n_new:(h + 1) * n_new, :].reshape(nb * n_new, o_ref.shape[2])
        out_ref[:, h * vdim:(h + 1) * vdim] = _bdot(x, wuv_ref[h]).astype(out_ref.dtype)


def _uv_sample(o_lat, wuv, layer, *, nb, n_new):
    bsz, rows, lora = o_lat.shape
    heads, vdim = wuv.shape[1], wuv.shape[3]
    return pl.pallas_call(
        functools.partial(_uv_kernel, n_new=n_new),
        grid=(bsz // nb,),
        in_specs=[pl.BlockSpec((nb, rows, lora), lambda i: (i, 0, 0)),
                  pl.BlockSpec((None,) + wuv.shape[1:], lambda i: (layer, 0, 0, 0))],
        out_specs=pl.BlockSpec((nb * n_new, heads * vdim), lambda i: (i, 0)),
        out_shape=jax.ShapeDtypeStruct((bsz * n_new, heads * vdim), BF16),
        compiler_params=_cparams("parallel"),
        name="uv_sample",
    )(o_lat, wuv)


def _gmlp_kernel(zc_ref, g_ref, b_ref, ws_ref, mask_ref, bias_ref, c_ref, v_ref):
    zc = zc_ref[...]
    zc = 0.5 * zc * (1.0 + lax.erf(zc * np.float32(np.sqrt(0.5))))
    dc = zc.shape[1] // 2
    u = zc[:, 0:dc]
    v = zc[:, dc:]
    vc = v - jnp.mean(v, axis=-1, keepdims=True)
    vn = vc * lax.rsqrt(jnp.mean(vc * vc, axis=-1, keepdims=True) + LN_EPS) * g_ref[...] + b_ref[...]
    v_ref[...] = vn
    mask = mask_ref[...]
    vb = vn.astype(BF16)
    groups = ws_ref.shape[0]
    gd = dc // groups
    for g in range(groups):
        cols = slice(g * gd, (g + 1) * gd)
        wm = jnp.where(mask > 0, ws_ref[g], 0.0).astype(BF16)
        for c in range(zc.shape[0] // CHUNK):
            rows = slice(c * CHUNK, (c + 1) * CHUNK)
            s = jnp.dot(wm, vb[rows, cols], preferred_element_type=F32) + bias_ref[:, cols]
            c_ref[rows, cols] = (u[rows, cols] * s).astype(c_ref.dtype)


def _gmlp(zc, ln_g, ln_b, ws, mask, bias, layer, *, tm, prompt_tiles):
    m, d2 = zc.shape
    dc = d2 // 2
    groups = ws.shape[2]

    def grp(i):
        return jnp.where(i < prompt_tiles, 0, 1)

    vec = pl.BlockSpec((None, 1, dc), lambda i: (layer, 0, 0))
    return pl.pallas_call(
        _gmlp_kernel,
        grid=(m // tm,),
        in_specs=[pl.BlockSpec((tm, d2), lambda i: (i, 0)), vec, vec,
                  pl.BlockSpec((None, None, groups, CHUNK, CHUNK), lambda i: (layer, grp(i), 0, 0, 0)),
                  pl.BlockSpec((None, CHUNK, CHUNK), lambda i: (grp(i), 0, 0)),
                  pl.BlockSpec((None, None, CHUNK, dc), lambda i: (layer, grp(i), 0, 0))],
        out_specs=[pl.BlockSpec((tm, dc), lambda i: (i, 0)), pl.BlockSpec((tm, dc), lambda i: (i, 0))],
        out_shape=[jax.ShapeDtypeStruct((m, dc), BF16), jax.ShapeDtypeStruct((m, dc), F32)],
        compiler_params=_cparams("parallel"),
        name="gmlp",
    )(zc, ln_g, ln_b, ws, mask, bias)


def _merge_kernel(a_ref, b_ref, c_ref, ga_ref, gb_ref, gc_ref, wa_ref, wb_ref, wc_ref, o_ref):
    m = (ga_ref[...] * jnp.dot(a_ref[...], wa_ref[...].astype(BF16), preferred_element_type=F32)
         + gb_ref[...] * jnp.dot(b_ref[...], wb_ref[...].astype(BF16), preferred_element_type=F32)
         + gc_ref[...] * jnp.dot(c_ref[...], wc_ref[...].astype(BF16), preferred_element_type=F32))
    o_ref[...] = m.astype(o_ref.dtype)


def _merge(a, b, c, sg, wa, wb, wc, layer, *, tm, tn):
    m, kd = a.shape
    n = wa.shape[2]
    nj = n // tn
    pre = pl.BlockSpec((tm, kd), lambda i, j: (i, 0))

    def gate(s):
        return pl.BlockSpec((tm, tn), lambda i, j: (i, s * nj + j))

    w = pl.BlockSpec((None, kd, tn), lambda i, j: (layer, 0, j))
    return pl.pallas_call(
        _merge_kernel,
        grid=(m // tm, nj),
        in_specs=[pre, pre, pre, gate(0), gate(1), gate(2), w, w, w],
        out_specs=pl.BlockSpec((tm, tn), lambda i, j: (i, j)),
        out_shape=jax.ShapeDtypeStruct((m, n), BF16),
        compiler_params=_cparams("parallel", "arbitrary"),
        name="merge",
    )(a, b, c, sg, sg, sg, wa, wb, wc)


def _wo_kernel(m_ref, w_ref, x_ref, g_ref, o_ref):
    kk = pl.program_id(1)

    @pl.when(kk == 0)
    def _():
        o_ref[...] = jnp.zeros(o_ref.shape, F32)

    o_ref[...] += jnp.dot(m_ref[...], w_ref[...].astype(BF16), preferred_element_type=F32)

    @pl.when(kk == pl.num_programs(1) - 1)
    def _():
        o_ref[...] = x_ref[...] + _rms(o_ref[...], g_ref[...])


def _wo_norm(mm, w, x, g, layer, *, tm, tk):
    m, d = x.shape
    return pl.pallas_call(
        _wo_kernel,
        grid=(m // tm, d // tk),
        in_specs=[pl.BlockSpec((tm, tk), lambda i, k: (i, k)),
                  pl.BlockSpec((None, tk, d), lambda i, k: (layer, k, 0)),
                  pl.BlockSpec((tm, d), lambda i, k: (i, 0)),
                  pl.BlockSpec((None, 1, d), lambda i, k: (layer, 0, 0))],
        out_specs=pl.BlockSpec((tm, d), lambda i, k: (i, 0)),
        out_shape=jax.ShapeDtypeStruct((m, d), F32),
        compiler_params=_cparams("parallel", "arbitrary"),
        name="wo_norm",
    )(mm, w, x, g)


def _ffn_kernel(x_ref, gpre_ref, up_ref, down_ref, gpost_ref, o_ref, hn_ref):
    f = pl.program_id(1)

    @pl.when(f == 0)
    def _():
        hn_ref[...] = _rms(x_ref[...], gpre_ref[...]).astype(BF16)
        o_ref[...] = jnp.zeros(o_ref.shape, F32)

    a = jnp.dot(hn_ref[...], up_ref[...].astype(BF16), preferred_element_type=F32)
    a = jnp.square(jnp.maximum(a, 0.0))
    o_ref[...] += jnp.dot(a.astype(BF16), down_ref[...].astype(BF16), preferred_element_type=F32)

    @pl.when(f == pl.num_programs(1) - 1)
    def _():
        o_ref[...] = x_ref[...] + _rms(o_ref[...], gpost_ref[...])


def _ffn(x, gpre, up, down, gpost, layer, *, tm, tf):
    m, d = x.shape
    dff = up.shape[2]
    vec = pl.BlockSpec((None, 1, d), lambda i, f: (layer, 0, 0))
    return pl.pallas_call(
        _ffn_kernel,
        grid=(m // tm, dff // tf),
        in_specs=[pl.BlockSpec((tm, d), lambda i, f: (i, 0)), vec,
                  pl.BlockSpec((None, d, tf), lambda i, f: (layer, 0, f)),
                  pl.BlockSpec((None, tf, d), lambda i, f: (layer, f, 0)), vec],
        out_specs=pl.BlockSpec((tm, d), lambda i, f: (i, 0)),
        out_shape=jax.ShapeDtypeStruct((m, d), F32),
        scratch_shapes=[pltpu.VMEM((tm, d), BF16)],
        compiler_params=_cparams("parallel", "arbitrary"),
        name="ffn",
    )(x, gpre, up, down, gpost)


def _ple_kernel(x_ref, xc_ref, p_ref, wg_ref, wp_ref, o_ref, xb_ref):
    @pl.when(pl.program_id(1) == 0)
    def _():
        xb_ref[...] = x_ref[...].astype(BF16)

    gate = jax.nn.sigmoid(jnp.dot(xb_ref[...], wg_ref[...].astype(BF16), preferred_element_type=F32))
    proj = _bdot(p_ref[...], wp_ref[...])
    o_ref[...] = xc_ref[...] + gate * proj


def _ple(x, p, wg, wp, layer, *, tm, tn):
    m, d = x.shape
    pd = p.shape[1]
    return pl.pallas_call(
        _ple_kernel,
        grid=(m // tm, d // tn),
        in_specs=[pl.BlockSpec((tm, d), lambda i, j: (i, 0)),
                  pl.BlockSpec((tm, tn), lambda i, j: (i, j)),
                  pl.BlockSpec((tm, pd), lambda i, j: (i, 0)),
                  pl.BlockSpec((None, d, tn), lambda i, j: (layer, 0, j)),
                  pl.BlockSpec((None, pd, tn), lambda i, j: (layer, 0, j))],
        out_specs=pl.BlockSpec((tm, tn), lambda i, j: (i, j)),
        out_shape=jax.ShapeDtypeStruct((m, d), F32),
        scratch_shapes=[pltpu.VMEM((tm, d), BF16)],
        compiler_params=_cparams("parallel", "arbitrary"),
        name="ple",
    )(x, x, p, wg, wp)


def _scan_consts():
    lane = np.arange(LANE)
    ones_bd = (lane[:, None] // A_HEAD == lane[None, :] // A_HEAD).astype(np.float32)
    eye = (lane[None, :] % A_HEAD == np.arange(A_HEAD)[:, None]).astype(np.float32)
    sel = np.zeros((16, LANE), np.float32)
    sel[0, :A_HEAD] = 1.0
    sel[1, A_HEAD:] = 1.0
    return jnp.asarray(ones_bd, BF16), eye, jnp.asarray(sel, BF16)


def _rope_tables(pos, rope):
    inv_freq = ROPE_THETA ** (-jnp.arange(0, rope, 2, dtype=F32) / rope)
    ang = pos.astype(F32)[:, None] * inv_freq[None, :]
    cos, sin = jnp.cos(ang), jnp.sin(ang)
    return jnp.concatenate([cos, cos], axis=-1), jnp.concatenate([sin, sin], axis=-1)


def _rot_half_cols(w):
    half = w.shape[-1] // 2
    return jnp.concatenate([-w[..., half:], w[..., :half]], axis=-1)


def kernel(x_prompt, x_sample, state_rwkv, state_rwkv_shift, cache_ckv, cache_kr, page_table,
           p_prompt, p_sample, norm_mix_pre, norm_mix_post, norm_ffn_pre, norm_ffn_post, w_in,
           rwkv_mu, rwkv_w0, rwkv_w2, rwkv_a0, rwkv_a2, rwkv_g2, rwkv_k_k, rwkv_k_a, rwkv_r_k,
           rwkv_lnx_g, rwkv_lnx_b, w_out_a, mla_q_norm, mla_w_uq, mla_kv_norm, mla_w_uk, mla_w_uv,
           w_out_b, gmlp_ln_g, gmlp_ln_b, gmlp_w_s, gmlp_b_s, w_out_c, w_o, ffn_up, ffn_down,
           ple_proj, ple_gate):
    depth = w_in.shape[0]
    bsz, seq, d = x_prompt.shape
    dbsz, dseq, _ = x_sample.shape
    mp, ms = bsz * seq, dbsz * dseq
    mt = mp + ms
    da = rwkv_w0.shape[1]
    heads_a = da // A_HEAD
    n_a_in = rwkv_mu.shape[1]
    lora = mla_q_norm.shape[1]
    kv_lora = mla_kv_norm.shape[1]
    assert lora == kv_lora
    b_heads, nope = mla_w_uk.shape[2], mla_w_uk.shape[3]
    rope = mla_w_uq.shape[3] - nope
    vdim = mla_w_uv.shape[3]
    dc = gmlp_ln_g.shape[1]
    groups = gmlp_w_s.shape[1]
    past = page_table.shape[1] * PAGE_SIZE
    scale = float((nope + rope) ** -0.5)
    n_b_in = lora + kv_lora + rope
    col_b = n_a_in
    col_c = n_a_in + n_b_in
    col_g = col_c + 2 * dc
    wa_ext = -(-n_a_in // 512) * 512
    assert 3 * da + 512 == wa_ext and col_c % 512 == 0 and col_g % 512 == 0

    def v3(a):
        return a.reshape(a.shape[0], 1, -1)

    kr0 = col_b + lora + kv_lora
    zpad = jnp.zeros((depth, d, LANE - rope), F32)
    w_b = jnp.concatenate([w_in[:, :, col_b:kr0], w_in[:, :, kr0:kr0 + rope], zpad,
                           _rot_half_cols(w_in[:, :, kr0:kr0 + rope]), zpad], axis=-1).astype(BF16)
    w_in = w_in.astype(BF16)
    ffn_up, ffn_down = ffn_up.astype(BF16), ffn_down.astype(BF16)
    w_o, ple_gate = w_o.astype(BF16), ple_gate.astype(BF16)
    w_out_a, w_out_b, w_out_c = w_out_a.astype(BF16), w_out_b.astype(BF16), w_out_c.astype(BF16)
    wl = jnp.zeros((depth, wa_ext - 3 * da, 3 * da), F32)
    wl = wl.at[:, 0:W_LORA, 0:da].set(rwkv_w2)
    wl = wl.at[:, W_LORA:W_LORA + A_LORA, da:2 * da].set(rwkv_a2)
    wl = wl.at[:, W_LORA + A_LORA:W_LORA + A_LORA + G_LORA, 2 * da:3 * da].set(rwkv_g2)
    wl = wl.astype(BF16)
    mu_ext = jnp.pad(rwkv_mu, ((0, 0), (0, wa_ext - n_a_in)))
    rwkv_vecs = {"mu": v3(mu_ext), "w0": v3(rwkv_w0), "a0": v3(rwkv_a0), "k_k": v3(rwkv_k_k),
                 "k_a": v3(rwkv_k_a), "r_k": v3(rwkv_r_k)}
    w_qn = mla_w_uq[..., :nope].reshape(depth, lora, b_heads * nope)
    w_qr = jnp.transpose(mla_w_uq[..., nope:], (0, 2, 1, 3))
    w_qrr = _rot_half_cols(w_qr)
    w_ukt = jnp.transpose(mla_w_uk, (0, 2, 3, 1))
    w_uvt = jnp.transpose(mla_w_uv, (0, 2, 1, 3))
    seqs_per_chunk = CHUNK // dseq
    tri = np.tril(np.ones((CHUNK, CHUNK), np.float32))
    blk = np.kron(np.eye(seqs_per_chunk, dtype=np.float32), np.tril(np.ones((dseq, dseq), np.float32)))
    sp_mask = jnp.asarray(np.stack([tri, blk]))
    ws_s = jnp.tile(gmlp_w_s[:, :, :dseq, :dseq], (1, 1, seqs_per_chunk, seqs_per_chunk))
    ws_all = jnp.stack([gmlp_w_s, ws_s], axis=1)
    gd = dc // groups
    bias_p = jnp.repeat(jnp.swapaxes(gmlp_b_s, 1, 2), gd, axis=2)
    bias_s = jnp.tile(bias_p[:, :dseq], (1, seqs_per_chunk, 1))
    bias_all = jnp.stack([bias_p, bias_s], axis=1)

    cs_p, sn_p = _rope_tables(jnp.arange(seq), rope)
    cs_s, sn_s = _rope_tables(past + jnp.arange(dseq), rope)
    cs = jnp.concatenate([jnp.tile(cs_p, (bsz, 1)), jnp.tile(cs_s, (dbsz, 1))], axis=0)
    sn = jnp.concatenate([jnp.tile(sn_p, (bsz, 1)), jnp.tile(sn_s, (dbsz, 1))], axis=0)
    ones_bd, eye, sel = _scan_consts()
    cache_krt = jnp.swapaxes(cache_kr, 2, 3)

    x = jnp.concatenate([x_prompt.reshape(mp, d), x_sample.reshape(ms, d)], axis=0)
    ple_all = jnp.concatenate([p_prompt.reshape(depth, mp, -1), p_sample.reshape(depth, ms, -1)], axis=1)

    tm_big = _pick_tile(mt, 1024)
    tm_mid = _pick_tile(mt, 512)
    tm_prep_p = _pick_tile(seq, 256, 8)
    tm_prep_s = _pick_tile(ms, 256, 8)
    tm_g = _pick_tile(int(np.gcd(mp, ms)), 512)
    tq = _pick_tile(seq, 128)
    tk = _pick_tile(seq, 512)
    nb_s = max(n for n in (4, 2, 1) if dbsz % n == 0)
    npg = max(n for n in (16, 8, 4, 2, 1) if page_table.shape[1] % n == 0)
    nb_uv = _pick_tile(dbsz, 16, 1)

    outs = {k: [] for k in ("st_p", "sh_p", "ckv_p", "kr_p", "st_s", "sh_s", "ckv_s", "kr_s", "vc_s")}
    for i in range(depth):
        g_pre = v3(norm_mix_pre)
        za, h = _norm_mm(x, g_pre, w_in, i, col0=0, ncols=wa_ext, tn=wa_ext // 4, tm=tm_big)
        zb = _mm(h, w_b, i, col0=0, ncols=w_b.shape[2], tn=w_b.shape[2] // 2, tm=tm_big)
        zc = _mm(h, w_in, i, col0=col_c, ncols=2 * dc, tn=512, tm=tm_big)
        sg = _mm(h, w_in, i, col0=col_g, ncols=3 * d, tn=512, tm=tm_big, act="sigmoid")

        shift0 = jnp.pad(state_rwkv_shift[i], ((0, 0), (0, wa_ext - n_a_in)))
        shift_rows = jnp.repeat(shift0, dseq, axis=0)
        prep_p = _rwkv_prep(za, za, rwkv_vecs, wl, ones_bd, i, row0=0, nrows=mp, tm=tm_prep_p,
                            seq_len=seq, aux_is_carry=True)
        prep_s = _rwkv_prep(za, shift_rows, rwkv_vecs, wl, ones_bd, i, row0=mp, nrows=ms,
                            tm=tm_prep_s, seq_len=dseq, aux_is_carry=False)
        def scan_consts(nb):
            return [ones_bd, jnp.asarray(np.tile(eye, (nb * heads_a // 2, 1))), sel]

        y_p, st_p = _rwkv_scan(None, [a.reshape(bsz, seq, da) for a in prep_p[:6]],
                               scan_consts(bsz), i, nb=bsz, tc=_pick_tile(seq, 16, 8))
        y_s, st_s = _rwkv_scan(state_rwkv, [a.reshape(dbsz, dseq, da) for a in prep_s[:6]],
                               scan_consts(nb_s), i, nb=nb_s, tc=dseq)
        a_p = _rwkv_post(y_p.reshape(mp, da), prep_p[7], prep_p[6], v3(rwkv_lnx_g), v3(rwkv_lnx_b),
                         ones_bd, i, tm=_pick_tile(mp, 512))
        a_s = _rwkv_post(y_s.reshape(ms, da), prep_s[7], prep_s[6], v3(rwkv_lnx_g), v3(rwkv_lnx_b),
                         ones_bd, i, tm=_pick_tile(ms, 512))
        a_pre = jnp.concatenate([a_p, a_s], axis=0)
        sh_p = jnp.concatenate([za[r:r + 1, :n_a_in] for r in range(seq - 1, mp, seq)], axis=0)
        sh_s = za[mp + dseq - 1::dseq, :n_a_in]

        q_all, k_all, ckv, kr = _mla_proj(zb, cs, sn, v3(mla_q_norm), v3(mla_kv_norm), w_qn, w_qr,
                                          w_qrr, w_ukt, i, tm=tm_mid, lora=lora, rope=rope)
        b_p = _flash_prompt(q_all, k_all, w_uvt, i, bsz=bsz, t=seq, tq=tq, tk=tk, lora=lora,
                            scale=scale)
        q_s = q_all[:, mp:].reshape(b_heads, dbsz, dseq, -1)
        q_s = jnp.transpose(q_s, (1, 0, 2, 3)).reshape(dbsz, b_heads * dseq, -1)
        knew = jnp.pad(k_all[mp:].reshape(dbsz, dseq, -1), ((0, 0), (0, 16 - dseq), (0, 0)))
        o_s = _paged_attend(page_table, q_s, cache_ckv, cache_krt, knew, i, npg=npg, lora=lora,
                            rope=rope, n_new=dseq, scale=scale)
        b_s = _uv_sample(o_s, w_uvt, i, nb=nb_uv, n_new=dseq)
        b_pre = jnp.concatenate([b_p, b_s], axis=0)

        c_pre, v_c = _gmlp(zc, v3(gmlp_ln_g), v3(gmlp_ln_b), ws_all, sp_mask, bias_all, i,
                           tm=tm_g, prompt_tiles=mp // tm_g)

        mm = _merge(a_pre, b_pre, c_pre, sg, w_out_a, w_out_b, w_out_c, i, tm=tm_big, tn=512)
        x = _wo_norm(mm, w_o, x, v3(norm_mix_post), i, tm=tm_mid, tk=d)
        x = _ffn(x, v3(norm_ffn_pre), ffn_up, ffn_down, v3(norm_ffn_post), i, tm=tm_big, tf=512)
        x = _ple(x, ple_all[i], ple_gate, ple_proj, i, tm=tm_big, tn=512)

        outs["st_p"].append(st_p)
        outs["sh_p"].append(sh_p)
        outs["ckv_p"].append(ckv[:mp].reshape(bsz, seq, kv_lora))
        outs["kr_p"].append(kr[:mp].reshape(bsz, seq, rope))
        outs["st_s"].append(st_s)
        outs["sh_s"].append(sh_s)
        outs["ckv_s"].append(ckv[mp:].reshape(dbsz, dseq, kv_lora))
        outs["kr_s"].append(kr[mp:].reshape(dbsz, dseq, rope))
        outs["vc_s"].append(v_c[mp:].reshape(dbsz, dseq, dc))

    return (x[:mp].reshape(bsz, seq, d), x[mp:].reshape(dbsz, dseq, d),
            jnp.stack(outs["st_p"]), jnp.stack(outs["sh_p"]), jnp.stack(outs["ckv_p"]),
            jnp.stack(outs["kr_p"]), jnp.stack(outs["st_s"]), jnp.stack(outs["sh_s"]),
            jnp.stack(outs["ckv_s"]), jnp.stack(outs["kr_s"]), jnp.stack(outs["vc_s"]))
```

```python
import functools

import jax
import jax.numpy as jnp
import numpy as np
from jax import lax
from jax.experimental import pallas as pl
from jax.experimental.pallas import tpu as pltpu

F32 = jnp.float32
BF16 = jnp.bfloat16

NORM_EPS = 1e-6
LNX_EPS = 64e-5
LN_EPS = 1e-5
ROPE_THETA = 10000.0
PAGE_SIZE = 128
CHUNK = 128
A_HEAD = 64
W_LORA = 96
A_LORA = 96
G_LORA = 256
LANE = 128
VMEM_LIMIT = 56 * 1024 * 1024


def _cparams(*sem):
    return pltpu.CompilerParams(dimension_semantics=sem, vmem_limit_bytes=VMEM_LIMIT)


def _pick_tile(m, target, quantum=LANE):
    t = min(target, m)
    t -= t % quantum
    while m % t:
        t -= quantum
    return t


def _rms(x, g):
    return x * lax.rsqrt(jnp.mean(x * x, axis=-1, keepdims=True) + NORM_EPS) * g


def _bdot(a, b):
    return jnp.dot(a.astype(BF16), b.astype(BF16), preferred_element_type=F32)


def _seg_sum(x, ones_bd):
    outs = []
    for c in range(x.shape[1] // LANE):
        xb = x[:, c * LANE:(c + 1) * LANE]
        hi = xb.astype(BF16)
        r1 = xb - hi.astype(F32)
        mid = r1.astype(BF16)
        lo = (r1 - mid.astype(F32)).astype(BF16)
        s = (jnp.dot(hi, ones_bd, preferred_element_type=F32)
             + jnp.dot(mid, ones_bd, preferred_element_type=F32)
             + jnp.dot(lo, ones_bd, preferred_element_type=F32))
        outs.append(s)
    return jnp.concatenate(outs, axis=1)


def _norm_mm_kernel(x_ref, g_ref, w_ref, o_ref, h_ref):
    @pl.when(pl.program_id(1) == 0)
    def _():
        h_ref[...] = _rms(x_ref[...], g_ref[...]).astype(BF16)

    o_ref[...] = jnp.dot(h_ref[...], w_ref[...].astype(BF16), preferred_element_type=F32)


def _norm_mm(x, g, w, layer, *, col0, ncols, tn, tm):
    m, k = x.shape
    j0 = col0 // tn
    assert col0 % tn == 0 and ncols % tn == 0
    return pl.pallas_call(
        _norm_mm_kernel,
        grid=(m // tm, ncols // tn),
        in_specs=[pl.BlockSpec((tm, k), lambda i, j: (i, 0)),
                  pl.BlockSpec((None, 1, k), lambda i, j: (layer, 0, 0)),
                  pl.BlockSpec((None, k, tn), lambda i, j: (layer, 0, j0 + j))],
        out_specs=[pl.BlockSpec((tm, tn), lambda i, j: (i, j)),
                   pl.BlockSpec((tm, k), lambda i, j: (i, 0))],
        out_shape=[jax.ShapeDtypeStruct((m, ncols), F32), jax.ShapeDtypeStruct((m, k), BF16)],
        compiler_params=_cparams("parallel", "arbitrary"),
        name="norm_mm",
    )(x, g, w)


def _mm_kernel(h_ref, w_ref, o_ref, *, sigmoid_from):
    y = jnp.dot(h_ref[...], w_ref[...].astype(BF16), preferred_element_type=F32)
    if sigmoid_from is None:
        o_ref[...] = y
    else:
        @pl.when(pl.program_id(1) < sigmoid_from)
        def _():
            o_ref[...] = y

        @pl.when(pl.program_id(1) >= sigmoid_from)
        def _():
            o_ref[...] = jax.nn.sigmoid(y)


def _mm(h, w, layer, *, col0, ncols, tn, tm, sigmoid_from=None):
    m, k = h.shape
    j0 = col0 // tn
    assert col0 % tn == 0 and ncols % tn == 0
    return pl.pallas_call(
        functools.partial(_mm_kernel, sigmoid_from=sigmoid_from),
        grid=(m // tm, ncols // tn),
        in_specs=[pl.BlockSpec((tm, k), lambda i, j: (i, 0)),
                  pl.BlockSpec((None, k, tn), lambda i, j: (layer, 0, j0 + j))],
        out_specs=pl.BlockSpec((tm, tn), lambda i, j: (i, j)),
        out_shape=jax.ShapeDtypeStruct((m, ncols), F32),
        compiler_params=_cparams("parallel", "parallel"),
        name="mm",
    )(h, w)


def _rwkv_prep_kernel(za_ref, aux_ref, mu_ref, wl_ref, w0_ref, a0_ref, kkw_ref, ka_ref, rk_ref,
                      ones_ref, r_ref, w_ref, k_ref, v_ref, nkk_ref, b_ref, gate_ref, bonus_ref,
                      *, seq_tiles, rows_per_seq):
    za = za_ref[...]
    tm = za.shape[0]
    row = lax.broadcasted_iota(jnp.int32, za.shape, 0)
    rolled = pltpu.roll(za, 1, 0)
    if seq_tiles is not None:
        first = jnp.where(pl.program_id(0) % seq_tiles == 0, 0.0, aux_ref[7:8, :])
        prev = jnp.where(row == 0, first, rolled)
    else:
        prev = jnp.where(row % rows_per_seq == 0, aux_ref[...], rolled)
    zs = za + mu_ref[...] * (prev - za)
    da = w0_ref.shape[1]
    r = zs[:, 0:da]
    k = zs[:, da:2 * da]
    v = zs[:, 2 * da:3 * da]
    lr = zs[:, 3 * da:]
    col = lax.broadcasted_iota(jnp.int32, lr.shape, 1)
    lact = jnp.where(col < W_LORA, jnp.tanh(lr),
                     jnp.where(col < W_LORA + A_LORA, lr, jax.nn.sigmoid(lr)))
    lo = jnp.dot(lact.astype(BF16), wl_ref[...], preferred_element_type=F32)
    w_log = -jax.nn.softplus(-(w0_ref[...] + lo[:, 0:da])) - 0.5
    decay = jnp.exp(-jnp.exp(w_log))
    a = jax.nn.sigmoid(a0_ref[...] + lo[:, da:2 * da])
    gate = lo[:, 2 * da:3 * da]
    ones_bd = ones_ref[...]
    kk = k * kkw_ref[...]
    kk = kk * lax.rsqrt(jnp.maximum(_seg_sum(kk * kk, ones_bd), 1e-24))
    k2 = k * (1.0 + (a - 1.0) * ka_ref[...])
    r_ref[...] = r
    w_ref[...] = decay
    k_ref[...] = k2
    v_ref[...] = v
    nkk_ref[...] = -kk
    b_ref[...] = kk * a
    gate_ref[...] = gate
    bonus_ref[...] = _seg_sum(r * k2 * rk_ref[...], ones_bd) * v


def _rwkv_prep(za_all, aux, vecs, wl, ones_bd, layer, *, row0, nrows, tm, seq_len, aux_is_carry):
    wa = za_all.shape[1]
    da = vecs["w0"].shape[-1]
    i0 = row0 // tm
    if aux_is_carry:
        seq_tiles = seq_len // tm
        c0 = row0 // 8
        aux_spec = pl.BlockSpec((8, wa), lambda i: (jnp.maximum(c0 + i * (tm // 8) - 1, 0), 0))
        rows_per_seq = None
    else:
        seq_tiles = None
        rows_per_seq = seq_len
        aux_spec = pl.BlockSpec((tm, wa), lambda i: (i, 0))

    def vec(n):
        return pl.BlockSpec((None, 1, n), lambda i: (layer, 0, 0))

    out_spec = pl.BlockSpec((tm, da), lambda i: (i, 0))
    out_sds = jax.ShapeDtypeStruct((nrows, da), F32)
    out_specs, out_shape = [out_spec] * 8, [out_sds] * 8
    return pl.pallas_call(
        functools.partial(_rwkv_prep_kernel, seq_tiles=seq_tiles, rows_per_seq=rows_per_seq),
        grid=(nrows // tm,),
        in_specs=[pl.BlockSpec((tm, wa), lambda i: (i0 + i, 0)), aux_spec, vec(wa),
                  pl.BlockSpec((None,) + wl.shape[1:], lambda i: (layer, 0, 0)),
                  vec(da), vec(da), vec(da), vec(da), vec(da),
                  pl.BlockSpec(ones_bd.shape, lambda i: (0, 0))],
        out_specs=out_specs,
        out_shape=out_shape,
        compiler_params=_cparams("parallel"),
        name="rwkv_prep",
    )(za_all, aux, vecs["mu"], wl, vecs["w0"], vecs["a0"], vecs["k_k"], vecs["k_a"], vecs["r_k"],
      ones_bd)


def _rwkv_scan_kernel(*refs, nb, zero_init):
    if zero_init:
        s0_ref = None
        (r_ref, w_ref, k_ref, v_ref, nkk_ref, b_ref, ones_ref, eye_ref, sel_ref,
         y_ref, so_ref, st_ref) = refs
    else:
        (s0_ref, r_ref, w_ref, k_ref, v_ref, nkk_ref, b_ref, ones_ref, eye_ref, sel_ref,
         y_ref, so_ref, st_ref) = refs
    ci = pl.program_id(1)
    tc = r_ref.shape[1]
    npair = r_ref.shape[2] // LANE
    nq = nb * npair

    @pl.when(ci == 0)
    def _():
        for bb in range(nb):
            for p in range(npair):
                rows = pl.ds((bb * npair + p) * A_HEAD, A_HEAD)
                if zero_init:
                    st_ref[rows, :] = jnp.zeros((A_HEAD, LANE), F32)
                else:
                    st_ref[rows, :] = jnp.concatenate(
                        [s0_ref[bb, 2 * p], s0_ref[bb, 2 * p + 1]], axis=1)

    ones_bd = ones_ref[...]
    eye = eye_ref[...]
    sel = sel_ref[...]
    low_half = lax.broadcasted_iota(jnp.int32, (1, LANE), 1) < A_HEAD

    def rows_of(ref, t):
        return jnp.concatenate(
            [jnp.broadcast_to(ref[bb, pl.ds(t, 1), pl.ds(p * LANE, LANE)], (A_HEAD, LANE))
             for bb in range(nb) for p in range(npair)], axis=0)

    for t in range(tc):
        s_old = st_ref[...]
        m1 = (s_old * rows_of(nkk_ref, t)).astype(BF16)
        sa = jnp.dot(m1, ones_bd, preferred_element_type=F32)
        d = (eye * rows_of(v_ref, t)).astype(BF16)
        v_bc = jnp.dot(d, ones_bd, preferred_element_type=F32)
        s_new = s_old * rows_of(w_ref, t) + sa * rows_of(b_ref, t) + v_bc * rows_of(k_ref, t)
        st_ref[...] = s_new
        m2 = (s_new * rows_of(r_ref, t)).astype(BF16)
        y16 = lax.dot_general(sel, m2, (((1,), (1,)), ((), ())), preferred_element_type=F32)
        for j in range(nq // 2):
            r0 = y16[0:1, j * LANE:(j + 1) * LANE]
            r1 = y16[1:2, j * LANE:(j + 1) * LANE]
            even = jnp.where(low_half, r0, pltpu.roll(r1, A_HEAD, 1))
            odd = jnp.where(low_half, pltpu.roll(r0, A_HEAD, 1), r1)
            for q, val in ((2 * j, even), (2 * j + 1, odd)):
                y_ref[q // npair, pl.ds(t, 1), pl.ds((q % npair) * LANE, LANE)] = val

    @pl.when(ci == pl.num_programs(1) - 1)
    def _():
        for bb in range(nb):
            for p in range(npair):
                s = st_ref[pl.ds((bb * npair + p) * A_HEAD, A_HEAD), :]
                so_ref[bb, 2 * p] = s[:, :A_HEAD]
                so_ref[bb, 2 * p + 1] = s[:, A_HEAD:]


def _rwkv_scan(s0, seqs, consts, layer, *, nb, tc):
    bsz, t, da = seqs[0].shape
    seq_spec = pl.BlockSpec((nb, tc, da), lambda g, c: (g, c, 0))
    heads = da // A_HEAD
    zero_init = s0 is None
    st_spec = pl.BlockSpec((nb, heads, A_HEAD, A_HEAD), lambda g, c: (g, 0, 0, 0))
    const_specs = [pl.BlockSpec(c.shape, lambda g, c_: (0, 0)) for c in consts]
    s0_spec = pl.BlockSpec((None, nb, heads, A_HEAD, A_HEAD), lambda g, c: (layer, g, 0, 0, 0))
    in_specs = ([] if zero_init else [s0_spec]) + [seq_spec] * 6 + const_specs
    args = ([] if zero_init else [s0]) + list(seqs) + list(consts)
    return pl.pallas_call(
        functools.partial(_rwkv_scan_kernel, nb=nb, zero_init=zero_init),
        grid=(bsz // nb, t // tc),
        in_specs=in_specs,
        out_specs=[seq_spec, st_spec],
        out_shape=[jax.ShapeDtypeStruct(seqs[0].shape, F32),
                   jax.ShapeDtypeStruct((bsz, heads, A_HEAD, A_HEAD), F32)],
        scratch_shapes=[pltpu.VMEM((nb * da // LANE * A_HEAD, LANE), F32)],
        compiler_params=_cparams("parallel", "arbitrary"),
        name="rwkv_scan",
    )(*args)


def _rwkv_post_kernel(y_ref, bonus_ref, gate_ref, g_ref, b_ref, ones_ref, o_ref):
    y = y_ref[...]
    ones_bd = ones_ref[...]
    yc = y - _seg_sum(y, ones_bd) * (1.0 / A_HEAD)
    var = _seg_sum(yc * yc, ones_bd) * (1.0 / A_HEAD)
    yn = yc * lax.rsqrt(var + LNX_EPS) * g_ref[...] + b_ref[...]
    o_ref[...] = ((yn + bonus_ref[...]) * gate_ref[...]).astype(o_ref.dtype)


def _rwkv_post(y, bonus, gate, lnx_g, lnx_b, ones_bd, layer, *, tm):
    m, da = bonus.shape
    row = pl.BlockSpec((tm, da), lambda i: (i, 0))
    y_spec = row
    vec = pl.BlockSpec((None, 1, da), lambda i: (layer, 0, 0))
    return pl.pallas_call(
        _rwkv_post_kernel,
        grid=(m // tm,),
        in_specs=[y_spec, row, row, vec, vec, pl.BlockSpec(ones_bd.shape, lambda i: (0, 0))],
        out_specs=row,
        out_shape=jax.ShapeDtypeStruct((m, da), BF16),
        compiler_params=_cparams("parallel"),
        name="rwkv_post",
    )(y, bonus, gate, lnx_g, lnx_b, ones_bd)


def _mla_proj_kernel(zb_ref, cs_ref, sn_ref, qn_ref, kvn_ref, wn_ref, wr_ref, wrr_ref, wuk_ref,
                     q_ref, kall_ref, ckv_ref, kr_ref, *, lora, rope):
    zb = zb_ref[...]
    cs = cs_ref[...]
    sn = sn_ref[...]
    cq = _rms(zb[:, 0:lora], qn_ref[...]).astype(BF16)
    ckv = _rms(zb[:, lora:2 * lora], kvn_ref[...])
    kr = zb[:, 2 * lora:2 * lora + rope] * cs + zb[:, 2 * lora + LANE:2 * lora + LANE + rope] * sn
    ckv_ref[...] = ckv
    kr_ref[...] = kr
    pad = jnp.zeros((zb.shape[0], LANE - rope), BF16)
    kall_ref[:, 0:lora] = ckv.astype(BF16)
    kall_ref[:, lora:lora + LANE] = jnp.concatenate([kr.astype(BF16), pad], axis=1)
    heads = wuk_ref.shape[0]
    nope = wuk_ref.shape[1]
    qn = jnp.dot(cq, wn_ref[...].astype(BF16), preferred_element_type=F32)
    for h in range(heads):
        q_lat = _bdot(qn[:, h * nope:(h + 1) * nope], wuk_ref[h])
        q_rope = (jnp.dot(cq, wr_ref[h].astype(BF16), preferred_element_type=F32) * cs
                  + jnp.dot(cq, wrr_ref[h].astype(BF16), preferred_element_type=F32) * sn)
        q_ref[h, :, 0:lora] = q_lat.astype(BF16)
        q_ref[h, :, lora:lora + LANE] = jnp.concatenate([q_rope.astype(BF16), pad], axis=1)


def _mla_proj(zb, cs, sn, qn, kvn, wn, wr, wrr, wuk, layer, *, tm, lora, rope):
    m = zb.shape[0]
    heads, nope = wuk.shape[1], wuk.shape[2]
    dq = lora + LANE

    def full(a):
        nd = a.ndim - 1
        return pl.BlockSpec((None,) + a.shape[1:], lambda i: (layer,) + (0,) * nd)

    return pl.pallas_call(
        functools.partial(_mla_proj_kernel, lora=lora, rope=rope),
        grid=(m // tm,),
        in_specs=[pl.BlockSpec((tm, zb.shape[1]), lambda i: (i, 0)),
                  pl.BlockSpec((tm, rope), lambda i: (i, 0)),
                  pl.BlockSpec((tm, rope), lambda i: (i, 0)),
                  full(qn), full(kvn), full(wn), full(wr), full(wrr), full(wuk)],
        out_specs=[pl.BlockSpec((heads, tm, dq), lambda i: (0, i, 0)),
                   pl.BlockSpec((tm, dq), lambda i: (i, 0)),
                   pl.BlockSpec((tm, lora), lambda i: (i, 0)),
                   pl.BlockSpec((tm, rope), lambda i: (i, 0))],
        out_shape=[jax.ShapeDtypeStruct((heads, m, dq), BF16),
                   jax.ShapeDtypeStruct((m, dq), BF16),
                   jax.ShapeDtypeStruct((m, lora), F32),
                   jax.ShapeDtypeStruct((m, rope), F32)],
        compiler_params=_cparams("parallel"),
        name="mla_proj",
    )(zb, cs, sn, qn, kvn, wn, wr, wrr, wuk)


NEG_BIG = -1e30
PAGED_SLOTS = 4


def _flash_kernel(q_ref, k_ref, wuv_ref, o_ref, m_ref, l_ref, acc_ref, *, tq, tk, lora, scale, nsplit):
    qi = pl.program_id(1)
    heads = q_ref.shape[0]
    rows = heads * tq
    m_ref[...] = jnp.full(m_ref.shape, NEG_BIG, F32)
    l_ref[...] = jnp.zeros(l_ref.shape, F32)
    acc_ref[...] = jnp.zeros(acc_ref.shape, F32)

    def tile(ki, masked):
        k = k_ref[pl.ds(pl.multiple_of(ki * tk, tk), tk), :]
        for part in range(nsplit):
            hs = heads // nsplit
            rs = pl.ds(part * hs * tq, hs * tq)
            q = q_ref[part * hs:(part + 1) * hs].reshape(hs * tq, q_ref.shape[2])
            s = lax.dot_general(q, k, (((1,), (1,)), ((), ())), preferred_element_type=F32) * scale
            if masked:
                qpos = qi * tq + lax.broadcasted_iota(jnp.int32, s.shape, 0) % tq
                kpos = ki * tk + lax.broadcasted_iota(jnp.int32, s.shape, 1)
                s = jnp.where(kpos <= qpos, s, NEG_BIG)
            m_prev = m_ref[rs, :]
            m_new = jnp.maximum(m_prev, jnp.max(s, axis=-1, keepdims=True))
            alpha = jnp.exp(m_prev - m_new)
            p = jnp.exp(s - m_new)
            l_ref[rs, :] = alpha * l_ref[rs, :] + jnp.sum(p, axis=-1, keepdims=True)
            acc_ref[rs, :] = alpha * acc_ref[rs, :] + jnp.dot(p.astype(BF16), k[:, 0:lora],
                                                              preferred_element_type=F32)
            m_ref[rs, :] = m_new

    n_full = (qi * tq) // tk

    def body(ki, carry):
        tile(ki, False)
        return carry

    lax.fori_loop(0, n_full, body, 0)
    tile(n_full, True)

    o = (acc_ref[...] / l_ref[...]).astype(BF16)
    vdim = wuv_ref.shape[2]
    for h in range(heads):
        o_ref[:, h * vdim:(h + 1) * vdim] = jnp.dot(
            o[h * tq:(h + 1) * tq], wuv_ref[h].astype(BF16),
            preferred_element_type=F32).astype(o_ref.dtype)


def _flash_prompt(q, kall, wuv, layer, *, bsz, t, tq, tk, lora, scale):
    heads, _, dq = q.shape
    vdim = wuv.shape[3]
    nq = t // tq
    assert tk % tq == 0 and t % tk == 0
    return pl.pallas_call(
        functools.partial(_flash_kernel, tq=tq, tk=tk, lora=lora, scale=scale,
                          nsplit=2 if heads % 2 == 0 else 1),
        grid=(bsz, nq),
        in_specs=[pl.BlockSpec((heads, tq, dq), lambda b, i: (0, b * nq + i, 0)),
                  pl.BlockSpec((t, dq), lambda b, i: (b, 0)),
                  pl.BlockSpec((None,) + wuv.shape[1:], lambda b, i: (layer, 0, 0, 0))],
        out_specs=pl.BlockSpec((tq, heads * vdim), lambda b, i: (b * nq + i, 0)),
        out_shape=jax.ShapeDtypeStruct((bsz * t, heads * vdim), BF16),
        scratch_shapes=[pltpu.VMEM((heads * tq, 1), F32), pltpu.VMEM((heads * tq, 1), F32),
                        pltpu.VMEM((heads * tq, lora), F32)],
        compiler_params=_cparams("parallel", "arbitrary"),
        name="flash_prompt",
    )(q, kall, wuv)


def _paged_kernel(pt_ref, q_ref, ckv_hbm, krt_hbm, knew_ref, o_ref, cbuf_ref, rbuf_ref, sem_ref,
                  c16_ref, s_ref, m_ref, l_ref, acc_ref, *, layer, npg, n_chunks, n_seqs, lora, rope,
                  n_new, scale):
    b = pl.program_id(0)
    q = q_ref[0]
    q_lat = q[:, 0:lora]
    q_rope = q[:, lora:lora + rope]

    n_slots = cbuf_ref.shape[0]
    ahead = n_slots - 1

    def slot_of(seq, chunk):
        return (seq * n_chunks + chunk) % n_slots

    def copies(seq, chunk):
        slot = slot_of(seq, chunk)
        out = []
        for mpg in range(npg):
            page = pt_ref[seq, chunk * npg + mpg]
            out.append(pltpu.make_async_copy(ckv_hbm.at[layer, page], cbuf_ref.at[slot, mpg],
                                             sem_ref.at[0, slot]))
            out.append(pltpu.make_async_copy(krt_hbm.at[layer, page], rbuf_ref.at[slot, mpg],
                                             sem_ref.at[1, slot]))
        return out

    def start_at(seq, offset):
        seq_off, chunk = divmod(offset, n_chunks)
        if isinstance(seq, int):
            if seq + seq_off < n_seqs:
                for c in copies(seq + seq_off, chunk):
                    c.start()
        else:
            @pl.when(seq + seq_off < n_seqs)
            def _():
                for c in copies(seq + seq_off, chunk):
                    c.start()

    @pl.when(b == 0)
    def _():
        for offset in range(ahead):
            start_at(0, offset)

    m_ref[...] = jnp.full(m_ref.shape, NEG_BIG, F32)
    l_ref[...] = jnp.zeros(l_ref.shape, F32)
    acc_ref[...] = jnp.zeros(acc_ref.shape, F32)

    def update(s, vals):
        m_prev = m_ref[...]
        m_new = jnp.maximum(m_prev, jnp.max(s, axis=-1, keepdims=True))
        alpha = jnp.exp(m_prev - m_new)
        p = jnp.exp(s - m_new)
        l_ref[...] = alpha * l_ref[...] + jnp.sum(p, axis=-1, keepdims=True)
        acc_ref[...] = alpha * acc_ref[...] + jnp.dot(p.astype(BF16), vals,
                                                      preferred_element_type=F32)
        m_ref[...] = m_new

    def scores(chunk):
        for c in copies(b, chunk):
            c.wait()
        slot = slot_of(b, chunk)
        cb = cbuf_ref[slot].reshape(npg * PAGE_SIZE, lora).astype(BF16)
        rb = jnp.concatenate([rbuf_ref[slot, mpg].astype(BF16) for mpg in range(npg)], axis=1)
        c16_ref[chunk % 2] = cb
        s_ref[chunk % 2] = (
            lax.dot_general(q_lat, cb, (((1,), (1,)), ((), ())), preferred_element_type=F32)
            + jnp.dot(q_rope, rb, preferred_element_type=F32)) * scale

    scores(0)
    for chunk in range(n_chunks):
        start_at(b, chunk + ahead)
        if chunk + 1 < n_chunks:
            scores(chunk + 1)
        update(s_ref[chunk % 2], c16_ref[chunk % 2])

    kn = knew_ref[0]
    sn = lax.dot_general(q, kn, (((1,), (1,)), ((), ())), preferred_element_type=F32) * scale
    t_q = lax.broadcasted_iota(jnp.int32, sn.shape, 0) % n_new
    t_k = lax.broadcasted_iota(jnp.int32, sn.shape, 1)
    sn = jnp.where(t_k <= t_q, sn, NEG_BIG)
    update(sn, kn[:, 0:lora])
    o_ref[0] = acc_ref[...] / l_ref[...]


def _paged_attend(page_table, q, cache_ckv, cache_krt, knew, layer, *, npg, lora, rope, n_new, scale):
    bsz, rows, dq = q.shape
    n_pages = page_table.shape[1]
    assert n_pages % npg == 0
    grid_spec = pltpu.PrefetchScalarGridSpec(
        num_scalar_prefetch=1,
        grid=(bsz,),
        in_specs=[pl.BlockSpec((1, rows, dq), lambda b, pt: (b, 0, 0)),
                  pl.BlockSpec(memory_space=pl.ANY),
                  pl.BlockSpec(memory_space=pl.ANY),
                  pl.BlockSpec((1,) + knew.shape[1:], lambda b, pt: (b, 0, 0))],
        out_specs=pl.BlockSpec((1, rows, lora), lambda b, pt: (b, 0, 0)),
        scratch_shapes=[pltpu.VMEM((PAGED_SLOTS, npg, PAGE_SIZE, lora), F32),
                        pltpu.VMEM((PAGED_SLOTS, npg, rope, PAGE_SIZE), F32),
                        pltpu.SemaphoreType.DMA((2, PAGED_SLOTS)),
                        pltpu.VMEM((2, npg * PAGE_SIZE, lora), BF16),
                        pltpu.VMEM((2, rows, npg * PAGE_SIZE), F32),
                        pltpu.VMEM((rows, 1), F32), pltpu.VMEM((rows, 1), F32),
                        pltpu.VMEM((rows, lora), F32)],
    )
    return pl.pallas_call(
        functools.partial(_paged_kernel, layer=layer, npg=npg, n_chunks=n_pages // npg, n_seqs=bsz,
                          lora=lora,
                          rope=rope, n_new=n_new, scale=scale),
        grid_spec=grid_spec,
        out_shape=jax.ShapeDtypeStruct((bsz, rows, lora), F32),
        compiler_params=_cparams("arbitrary"),
        name="paged_attend",
    )(page_table, q, cache_ckv, cache_krt, knew)


def _uv_kernel(o_ref, wuv_ref, out_ref, *, n_new):
    heads, _, vdim = wuv_ref.shape
    nb = o_ref.shape[0]
    for h in range(heads):
        x = o_ref[:, h * n_new:(h + 1) * n_new, :].reshape(nb * n_new, o_ref.shape[2])
        out_ref[:, h * vdim:(h + 1) * vdim] = _bdot(x, wuv_ref[h]).astype(out_ref.dtype)


def _uv_sample(o_lat, wuv, layer, *, nb, n_new):
    bsz, rows, lora = o_lat.shape
    heads, vdim = wuv.shape[1], wuv.shape[3]
    return pl.pallas_call(
        functools.partial(_uv_kernel, n_new=n_new),
        grid=(bsz // nb,),
        in_specs=[pl.BlockSpec((nb, rows, lora), lambda i: (i, 0, 0)),
                  pl.BlockSpec((None,) + wuv.shape[1:], lambda i: (layer, 0, 0, 0))],
        out_specs=pl.BlockSpec((nb * n_new, heads * vdim), lambda i: (i, 0)),
        out_shape=jax.ShapeDtypeStruct((bsz * n_new, heads * vdim), BF16),
        compiler_params=_cparams("parallel"),
        name="uv_sample",
    )(o_lat, wuv)


def _gmlp_kernel(zc_ref, g_ref, b_ref, ws_ref, mask_ref, bias_ref, c_ref, v_ref):
    zc = zc_ref[...]
    zc = 0.5 * zc * (1.0 + lax.erf(zc * np.float32(np.sqrt(0.5))))
    dc = zc.shape[1] // 2
    u = zc[:, 0:dc]
    v = zc[:, dc:]
    vc = v - jnp.mean(v, axis=-1, keepdims=True)
    vn = vc * lax.rsqrt(jnp.mean(vc * vc, axis=-1, keepdims=True) + LN_EPS) * g_ref[...] + b_ref[...]
    v_ref[...] = vn
    mask = mask_ref[...]
    vb = vn.astype(BF16)
    groups = ws_ref.shape[0]
    gd = dc // groups
    for g in range(groups):
        cols = slice(g * gd, (g + 1) * gd)
        wm = jnp.where(mask > 0, ws_ref[g], 0.0).astype(BF16)
        for c in range(zc.shape[0] // CHUNK):
            rows = slice(c * CHUNK, (c + 1) * CHUNK)
            s = jnp.dot(wm, vb[rows, cols], preferred_element_type=F32) + bias_ref[:, cols]
            c_ref[rows, cols] = (u[rows, cols] * s).astype(c_ref.dtype)


def _gmlp(zc, ln_g, ln_b, ws, mask, bias, layer, *, tm, prompt_tiles):
    m = zc.shape[0]
    dc = ln_g.shape[-1]
    d2 = 2 * dc
    groups = ws.shape[2]

    def grp(i):
        return jnp.where(i < prompt_tiles, 0, 1)

    vec = pl.BlockSpec((None, 1, dc), lambda i: (layer, 0, 0))
    return pl.pallas_call(
        _gmlp_kernel,
        grid=(m // tm,),
        in_specs=[pl.BlockSpec((tm, d2), lambda i: (i, 0)), vec, vec,
                  pl.BlockSpec((None, None, groups, CHUNK, CHUNK), lambda i: (layer, grp(i), 0, 0, 0)),
                  pl.BlockSpec((None, CHUNK, CHUNK), lambda i: (grp(i), 0, 0)),
                  pl.BlockSpec((None, None, CHUNK, dc), lambda i: (layer, grp(i), 0, 0))],
        out_specs=[pl.BlockSpec((tm, dc), lambda i: (i, 0)), pl.BlockSpec((tm, dc), lambda i: (i, 0))],
        out_shape=[jax.ShapeDtypeStruct((m, dc), BF16), jax.ShapeDtypeStruct((m, dc), F32)],
        compiler_params=_cparams("parallel"),
        name="gmlp",
    )(zc, ln_g, ln_b, ws, mask, bias)


def _merge_kernel(a_ref, b_ref, c_ref, ga_ref, gb_ref, gc_ref, wa_ref, wb_ref, wc_ref, o_ref):
    m = (ga_ref[...] * jnp.dot(a_ref[...], wa_ref[...].astype(BF16), preferred_element_type=F32)
         + gb_ref[...] * jnp.dot(b_ref[...], wb_ref[...].astype(BF16), preferred_element_type=F32)
         + gc_ref[...] * jnp.dot(c_ref[...], wc_ref[...].astype(BF16), preferred_element_type=F32))
    o_ref[...] = m.astype(o_ref.dtype)


def _merge(a, b, c, sg, wa, wb, wc, layer, *, tm, tn, gate_col0):
    m, kd = a.shape
    n = wa.shape[2]
    nj = n // tn
    pre = pl.BlockSpec((tm, kd), lambda i, j: (i, 0))

    g0 = gate_col0 // tn

    def gate(s):
        return pl.BlockSpec((tm, tn), lambda i, j: (i, g0 + s * nj + j))

    w = pl.BlockSpec((None, kd, tn), lambda i, j: (layer, 0, j))
    return pl.pallas_call(
        _merge_kernel,
        grid=(m // tm, nj),
        in_specs=[pre, pre, pre, gate(0), gate(1), gate(2), w, w, w],
        out_specs=pl.BlockSpec((tm, tn), lambda i, j: (i, j)),
        out_shape=jax.ShapeDtypeStruct((m, n), BF16),
        compiler_params=_cparams("parallel", "arbitrary"),
        name="merge",
    )(a, b, c, sg, sg, sg, wa, wb, wc)


def _wo_kernel(m_ref, w_ref, x_ref, g_ref, o_ref):
    kk = pl.program_id(1)

    @pl.when(kk == 0)
    def _():
        o_ref[...] = jnp.zeros(o_ref.shape, F32)

    o_ref[...] += jnp.dot(m_ref[...], w_ref[...].astype(BF16), preferred_element_type=F32)

    @pl.when(kk == pl.num_programs(1) - 1)
    def _():
        o_ref[...] = x_ref[...] + _rms(o_ref[...], g_ref[...])


def _wo_norm(mm, w, x, g, layer, *, tm, tk):
    m, d = x.shape
    return pl.pallas_call(
        _wo_kernel,
        grid=(m // tm, d // tk),
        in_specs=[pl.BlockSpec((tm, tk), lambda i, k: (i, k)),
                  pl.BlockSpec((None, tk, d), lambda i, k: (layer, k, 0)),
                  pl.BlockSpec((tm, d), lambda i, k: (i, 0)),
                  pl.BlockSpec((None, 1, d), lambda i, k: (layer, 0, 0))],
        out_specs=pl.BlockSpec((tm, d), lambda i, k: (i, 0)),
        out_shape=jax.ShapeDtypeStruct((m, d), F32),
        compiler_params=_cparams("parallel", "arbitrary"),
        name="wo_norm",
    )(mm, w, x, g)


def _ffn_kernel(x_ref, gpre_ref, up_ref, down_ref, gpost_ref, o_ref, hn_ref):
    f = pl.program_id(1)

    @pl.when(f == 0)
    def _():
        hn_ref[...] = _rms(x_ref[...], gpre_ref[...]).astype(BF16)
        o_ref[...] = jnp.zeros(o_ref.shape, F32)

    a = jnp.dot(hn_ref[...], up_ref[...].astype(BF16), preferred_element_type=F32)
    a = jnp.square(jnp.maximum(a, 0.0))
    o_ref[...] += jnp.dot(a.astype(BF16), down_ref[...].astype(BF16), preferred_element_type=F32)

    @pl.when(f == pl.num_programs(1) - 1)
    def _():
        o_ref[...] = x_ref[...] + _rms(o_ref[...], gpost_ref[...])


def _ffn(x, gpre, up, down, gpost, layer, *, tm, tf):
    m, d = x.shape
    dff = up.shape[2]
    vec = pl.BlockSpec((None, 1, d), lambda i, f: (layer, 0, 0))
    return pl.pallas_call(
        _ffn_kernel,
        grid=(m // tm, dff // tf),
        in_specs=[pl.BlockSpec((tm, d), lambda i, f: (i, 0)), vec,
                  pl.BlockSpec((None, d, tf), lambda i, f: (layer, 0, f)),
                  pl.BlockSpec((None, tf, d), lambda i, f: (layer, f, 0)), vec],
        out_specs=pl.BlockSpec((tm, d), lambda i, f: (i, 0)),
        out_shape=jax.ShapeDtypeStruct((m, d), F32),
        scratch_shapes=[pltpu.VMEM((tm, d), BF16)],
        compiler_params=_cparams("parallel", "arbitrary"),
        name="ffn",
    )(x, gpre, up, down, gpost)


def _ple_kernel(x_ref, xc_ref, p_ref, wg_ref, wp_ref, o_ref, xb_ref):
    @pl.when(pl.program_id(1) == 0)
    def _():
        xb_ref[...] = x_ref[...].astype(BF16)

    gate = jax.nn.sigmoid(jnp.dot(xb_ref[...], wg_ref[...].astype(BF16), preferred_element_type=F32))
    proj = _bdot(p_ref[...], wp_ref[...])
    o_ref[...] = xc_ref[...] + gate * proj


def _ple(x, p, wg, wp, layer, *, tm, tn):
    m, d = x.shape
    pd = p.shape[1]
    return pl.pallas_call(
        _ple_kernel,
        grid=(m // tm, d // tn),
        in_specs=[pl.BlockSpec((tm, d), lambda i, j: (i, 0)),
                  pl.BlockSpec((tm, tn), lambda i, j: (i, j)),
                  pl.BlockSpec((tm, pd), lambda i, j: (i, 0)),
                  pl.BlockSpec((None, d, tn), lambda i, j: (layer, 0, j)),
                  pl.BlockSpec((None, pd, tn), lambda i, j: (layer, 0, j))],
        out_specs=pl.BlockSpec((tm, tn), lambda i, j: (i, j)),
        out_shape=jax.ShapeDtypeStruct((m, d), F32),
        scratch_shapes=[pltpu.VMEM((tm, d), BF16)],
        compiler_params=_cparams("parallel", "arbitrary"),
        name="ple",
    )(x, x, p, wg, wp)


def _scan_consts():
    lane = np.arange(LANE)
    ones_bd = (lane[:, None] // A_HEAD == lane[None, :] // A_HEAD).astype(np.float32)
    eye = (lane[None, :] % A_HEAD == np.arange(A_HEAD)[:, None]).astype(np.float32)
    sel = np.zeros((16, LANE), np.float32)
    sel[0, :A_HEAD] = 1.0
    sel[1, A_HEAD:] = 1.0
    return jnp.asarray(ones_bd, BF16), eye, jnp.asarray(sel, BF16)


def _rope_tables(pos, rope):
    inv_freq = ROPE_THETA ** (-jnp.arange(0, rope, 2, dtype=F32) / rope)
    ang = pos.astype(F32)[:, None] * inv_freq[None, :]
    cos, sin = jnp.cos(ang), jnp.sin(ang)
    return jnp.concatenate([cos, cos], axis=-1), jnp.concatenate([sin, sin], axis=-1)


def _rot_half_cols(w):
    half = w.shape[-1] // 2
    return jnp.concatenate([-w[..., half:], w[..., :half]], axis=-1)


def kernel(x_prompt, x_sample, state_rwkv, state_rwkv_shift, cache_ckv, cache_kr, page_table,
           p_prompt, p_sample, norm_mix_pre, norm_mix_post, norm_ffn_pre, norm_ffn_post, w_in,
           rwkv_mu, rwkv_w0, rwkv_w2, rwkv_a0, rwkv_a2, rwkv_g2, rwkv_k_k, rwkv_k_a, rwkv_r_k,
           rwkv_lnx_g, rwkv_lnx_b, w_out_a, mla_q_norm, mla_w_uq, mla_kv_norm, mla_w_uk, mla_w_uv,
           w_out_b, gmlp_ln_g, gmlp_ln_b, gmlp_w_s, gmlp_b_s, w_out_c, w_o, ffn_up, ffn_down,
           ple_proj, ple_gate):
    depth = w_in.shape[0]
    bsz, seq, d = x_prompt.shape
    dbsz, dseq, _ = x_sample.shape
    mp, ms = bsz * seq, dbsz * dseq
    mt = mp + ms
    da = rwkv_w0.shape[1]
    heads_a = da // A_HEAD
    n_a_in = rwkv_mu.shape[1]
    lora = mla_q_norm.shape[1]
    kv_lora = mla_kv_norm.shape[1]
    assert lora == kv_lora
    b_heads, nope = mla_w_uk.shape[2], mla_w_uk.shape[3]
    rope = mla_w_uq.shape[3] - nope
    vdim = mla_w_uv.shape[3]
    dc = gmlp_ln_g.shape[1]
    groups = gmlp_w_s.shape[1]
    past = page_table.shape[1] * PAGE_SIZE
    scale = float((nope + rope) ** -0.5)
    n_b_in = lora + kv_lora + rope
    col_b = n_a_in
    col_c = n_a_in + n_b_in
    col_g = col_c + 2 * dc
    wa_ext = -(-n_a_in // 512) * 512
    assert 3 * da + 512 == wa_ext and col_c % 512 == 0 and col_g % 512 == 0

    def v3(a):
        return a.reshape(a.shape[0], 1, -1)

    kr0 = col_b + lora + kv_lora
    zpad = jnp.zeros((depth, d, LANE - rope), F32)
    w_b = jnp.concatenate([w_in[:, :, col_b:kr0], w_in[:, :, kr0:kr0 + rope], zpad,
                           _rot_half_cols(w_in[:, :, kr0:kr0 + rope]), zpad], axis=-1).astype(BF16)
    w_cg = w_in[:, :, col_c:].astype(BF16)
    w_in = w_in[:, :, :wa_ext].astype(BF16)
    ffn_up, ffn_down = ffn_up.astype(BF16), ffn_down.astype(BF16)
    w_o, ple_gate = w_o.astype(BF16), ple_gate.astype(BF16)
    w_out_a, w_out_b, w_out_c = w_out_a.astype(BF16), w_out_b.astype(BF16), w_out_c.astype(BF16)
    wl = jnp.zeros((depth, wa_ext - 3 * da, 3 * da), F32)
    wl = wl.at[:, 0:W_LORA, 0:da].set(rwkv_w2)
    wl = wl.at[:, W_LORA:W_LORA + A_LORA, da:2 * da].set(rwkv_a2)
    wl = wl.at[:, W_LORA + A_LORA:W_LORA + A_LORA + G_LORA, 2 * da:3 * da].set(rwkv_g2)
    wl = wl.astype(BF16)
    mu_ext = jnp.pad(rwkv_mu, ((0, 0), (0, wa_ext - n_a_in)))
    rwkv_vecs = {"mu": v3(mu_ext), "w0": v3(rwkv_w0), "a0": v3(rwkv_a0), "k_k": v3(rwkv_k_k),
                 "k_a": v3(rwkv_k_a), "r_k": v3(rwkv_r_k)}
    w_qn = mla_w_uq[..., :nope].reshape(depth, lora, b_heads * nope)
    w_qr = jnp.transpose(mla_w_uq[..., nope:], (0, 2, 1, 3))
    w_qrr = _rot_half_cols(w_qr)
    w_ukt = jnp.transpose(mla_w_uk, (0, 2, 3, 1))
    w_uvt = jnp.transpose(mla_w_uv, (0, 2, 1, 3))
    seqs_per_chunk = CHUNK // dseq
    tri = np.tril(np.ones((CHUNK, CHUNK), np.float32))
    blk = np.kron(np.eye(seqs_per_chunk, dtype=np.float32), np.tril(np.ones((dseq, dseq), np.float32)))
    sp_mask = jnp.asarray(np.stack([tri, blk]))
    ws_s = jnp.tile(gmlp_w_s[:, :, :dseq, :dseq], (1, 1, seqs_per_chunk, seqs_per_chunk))
    ws_all = jnp.stack([gmlp_w_s, ws_s], axis=1)
    gd = dc // groups
    bias_p = jnp.repeat(jnp.swapaxes(gmlp_b_s, 1, 2), gd, axis=2)
    bias_s = jnp.tile(bias_p[:, :dseq], (1, seqs_per_chunk, 1))
    bias_all = jnp.stack([bias_p, bias_s], axis=1)

    cs_p, sn_p = _rope_tables(jnp.arange(seq), rope)
    cs_s, sn_s = _rope_tables(past + jnp.arange(dseq), rope)
    cs = jnp.concatenate([jnp.tile(cs_p, (bsz, 1)), jnp.tile(cs_s, (dbsz, 1))], axis=0)
    sn = jnp.concatenate([jnp.tile(sn_p, (bsz, 1)), jnp.tile(sn_s, (dbsz, 1))], axis=0)
    ones_bd, eye, sel = _scan_consts()
    cache_krt = jnp.swapaxes(cache_kr, 2, 3)

    x = jnp.concatenate([x_prompt.reshape(mp, d), x_sample.reshape(ms, d)], axis=0)
    ple_all = jnp.concatenate([p_prompt.reshape(depth, mp, -1), p_sample.reshape(depth, ms, -1)], axis=1)

    tm_big = _pick_tile(mt, 1024)
    tm_mid = _pick_tile(mt, 512)
    tm_prep_p = _pick_tile(seq, 256, 8)
    tm_prep_s = _pick_tile(ms, 256, 8)
    tm_g = _pick_tile(int(np.gcd(mp, ms)), 512)
    tq = _pick_tile(seq, 128)
    tk = _pick_tile(seq, 512)
    nb_s = max(n for n in (4, 2, 1) if dbsz % n == 0)
    npg = max(n for n in (16, 8, 4, 2, 1) if page_table.shape[1] % n == 0)
    nb_uv = _pick_tile(dbsz, 16, 1)

    outs = {k: [] for k in ("st_p", "sh_p", "ckv_p", "kr_p", "st_s", "sh_s", "ckv_s", "kr_s", "vc_s")}
    for i in range(depth):
        g_pre = v3(norm_mix_pre)
        za, h = _norm_mm(x, g_pre, w_in, i, col0=0, ncols=wa_ext, tn=wa_ext // 4, tm=tm_big)
        zb = _mm(h, w_b, i, col0=0, ncols=w_b.shape[2], tn=w_b.shape[2] // 2, tm=tm_big)
        zcg = _mm(h, w_cg, i, col0=0, ncols=w_cg.shape[2], tn=1024, tm=tm_big,
                  sigmoid_from=2 * dc // 1024)

        shift0 = jnp.pad(state_rwkv_shift[i], ((0, 0), (0, wa_ext - n_a_in)))
        shift_rows = jnp.repeat(shift0, dseq, axis=0)
        prep_p = _rwkv_prep(za, za, rwkv_vecs, wl, ones_bd, i, row0=0, nrows=mp, tm=tm_prep_p,
                            seq_len=seq, aux_is_carry=True)
        prep_s = _rwkv_prep(za, shift_rows, rwkv_vecs, wl, ones_bd, i, row0=mp, nrows=ms,
                            tm=tm_prep_s, seq_len=dseq, aux_is_carry=False)
        def scan_consts(nb):
            return [ones_bd, jnp.asarray(np.tile(eye, (nb * heads_a // 2, 1))), sel]

        y_p, st_p = _rwkv_scan(None, [a.reshape(bsz, seq, da) for a in prep_p[:6]],
                               scan_consts(bsz), i, nb=bsz, tc=_pick_tile(seq, 16, 8))
        y_s, st_s = _rwkv_scan(state_rwkv, [a.reshape(dbsz, dseq, da) for a in prep_s[:6]],
                               scan_consts(nb_s), i, nb=nb_s, tc=dseq)
        a_p = _rwkv_post(y_p.reshape(mp, da), prep_p[7], prep_p[6], v3(rwkv_lnx_g), v3(rwkv_lnx_b),
                         ones_bd, i, tm=_pick_tile(mp, 512))
        a_s = _rwkv_post(y_s.reshape(ms, da), prep_s[7], prep_s[6], v3(rwkv_lnx_g), v3(rwkv_lnx_b),
                         ones_bd, i, tm=_pick_tile(ms, 512))
        a_pre = jnp.concatenate([a_p, a_s], axis=0)
        sh_p = jnp.concatenate([za[r:r + 1, :n_a_in] for r in range(seq - 1, mp, seq)], axis=0)
        sh_s = za[mp + dseq - 1::dseq, :n_a_in]

        q_all, k_all, ckv, kr = _mla_proj(zb, cs, sn, v3(mla_q_norm), v3(mla_kv_norm), w_qn, w_qr,
                                          w_qrr, w_ukt, i, tm=tm_mid, lora=lora, rope=rope)
        b_p = _flash_prompt(q_all, k_all, w_uvt, i, bsz=bsz, t=seq, tq=tq, tk=tk, lora=lora,
                            scale=scale)
        q_s = q_all[:, mp:].reshape(b_heads, dbsz, dseq, -1)
        q_s = jnp.transpose(q_s, (1, 0, 2, 3)).reshape(dbsz, b_heads * dseq, -1)
        knew = jnp.pad(k_all[mp:].reshape(dbsz, dseq, -1), ((0, 0), (0, 16 - dseq), (0, 0)))
        o_s = _paged_attend(page_table, q_s, cache_ckv, cache_krt, knew, i, npg=npg, lora=lora,
                            rope=rope, n_new=dseq, scale=scale)
        b_s = _uv_sample(o_s, w_uvt, i, nb=nb_uv, n_new=dseq)
        b_pre = jnp.concatenate([b_p, b_s], axis=0)

        c_pre, v_c = _gmlp(zcg, v3(gmlp_ln_g), v3(gmlp_ln_b), ws_all, sp_mask, bias_all, i,
                           tm=tm_g, prompt_tiles=mp // tm_g)

        mm = _merge(a_pre, b_pre, c_pre, zcg, w_out_a, w_out_b, w_out_c, i, tm=tm_big, tn=512,
                    gate_col0=2 * dc)
        x = _wo_norm(mm, w_o, x, v3(norm_mix_post), i, tm=tm_mid, tk=d)
        x = _ffn(x, v3(norm_ffn_pre), ffn_up, ffn_down, v3(norm_ffn_post), i, tm=tm_big, tf=512)
        x = _ple(x, ple_all[i], ple_gate, ple_proj, i, tm=tm_big, tn=512)

        outs["st_p"].append(st_p)
        outs["sh_p"].append(sh_p)
        outs["ckv_p"].append(ckv[:mp].reshape(bsz, seq, kv_lora))
        outs["kr_p"].append(kr[:mp].reshape(bsz, seq, rope))
        outs["st_s"].append(st_s)
        outs["sh_s"].append(sh_s)
        outs["ckv_s"].append(ckv[mp:].reshape(dbsz, dseq, kv_lora))
        outs["kr_s"].append(kr[mp:].reshape(dbsz, dseq, rope))
        outs["vc_s"].append(v_c[mp:].reshape(dbsz, dseq, dc))

    return (x[:mp].reshape(bsz, seq, d), x[mp:].reshape(dbsz, dseq, d),
            jnp.stack(outs["st_p"]), jnp.stack(outs["sh_p"]), jnp.stack(outs["ckv_p"]),
            jnp.stack(outs["kr_p"]), jnp.stack(outs["st_s"]), jnp.stack(outs["sh_s"]),
            jnp.stack(outs["ckv_s"]), jnp.stack(outs["kr_s"]), jnp.stack(outs["vc_s"]))
```
